```python
import math
import jax, jax.numpy as jnp
from jax import lax
import numpy as np

D_MODEL = 1024
BATCH = 32
SEQ = 2048
DEPTH = 4
DEC_BATCH = 32
DEC_SEQ = 16
PAST_LEN = 4096

CHUNK = 64
Q_BLOCK = 128
N_MIX_LAYERS = (DEPTH + 1) // 2
N_POOL_LAYERS = DEPTH // 2

A_DK = 128
A_DV = 128
A_HEADS = (D_MODEL // 2) // A_DV
A_W = A_HEADS * A_DK
B_DQK = 64
B_DV = 2 * B_DQK
B_HEADS = (D_MODEL // 2) // B_DV
B_W = B_HEADS * B_DV
IN_COLS = 4 * A_W + 3 * B_W
MIX_W = A_HEADS * A_DV + B_W
POOL_WINDOWS = (2, 4, 8, 16)
POOL_GROUP = D_MODEL // len(POOL_WINDOWS)
POOL_HIST = max(POOL_WINDOWS) - 1
D_FF = 256 * math.ceil(8 * D_MODEL / 3 / 256)
CONV_W = 3
NUM_BUCKETS = 32
MAX_DISTANCE = 128
DN_ALPHA = (2 * DEPTH) ** 0.25
DN_BETA = (8 * DEPTH) ** -0.25
LN_EPS = 1e-5
RMS_EPS = 1e-6

kernel_name = "hgrn2_diffattn_pool_convffn_stream"


def layer_norm(x, g, b):
    xf = x.astype(jnp.float32)
    mu = jnp.mean(xf, -1, keepdims=True)
    var = jnp.mean(jnp.square(xf - mu), -1, keepdims=True)
    return ((xf - mu) * lax.rsqrt(var + LN_EPS) * g.astype(jnp.float32) + b.astype(jnp.float32)).astype(x.dtype)


def rms_norm(x, g):
    xf = x.astype(jnp.float32)
    return (xf * lax.rsqrt(jnp.mean(xf * xf, -1, keepdims=True) + RMS_EPS) * g.astype(jnp.float32)).astype(x.dtype)


def rel_bucket(rel):
    half = NUM_BUCKETS // 2
    max_exact = half // 2
    base = jnp.where(rel > 0, half, 0)
    n = jnp.abs(rel)
    nf = jnp.maximum(n, 1).astype(jnp.float32)
    large = max_exact + (jnp.log(nf / max_exact) / math.log(MAX_DISTANCE / max_exact)
                         * (half - max_exact)).astype(jnp.int32)
    large = jnp.minimum(large, half - 1)
    return base + jnp.where(n < max_exact, n, large)


def diff_attn(q, k, v, q_pos, k_pos, lam, rel_bias):
    scale = B_DQK ** -0.5
    visible = (k_pos[None, :] // CHUNK) <= (q_pos[:, None] // CHUNK)
    bias = jnp.moveaxis(rel_bias[rel_bucket(k_pos[None, :] - q_pos[:, None])], -1, 0).astype(jnp.float32)
    neg = jnp.finfo(jnp.float32).min

    def probs(qa, ka):
        s = jnp.einsum('bqhd,bkhd->bhqk', qa, ka, preferred_element_type=jnp.float32) * scale + bias
        return jax.nn.softmax(jnp.where(visible, s, neg), axis=-1)

    attn = probs(q[..., :B_DQK], k[..., :B_DQK]) - lam * probs(q[..., B_DQK:], k[..., B_DQK:])
    return jnp.einsum('bhqk,bkhd->bqhd', attn.astype(v.dtype), v)


def hgrn2_chunk(S0, q, log_f, k, v):
    L = q.shape[1]
    b = jnp.cumsum(log_f, axis=1)
    causal = jnp.tril(jnp.ones((L, L), bool))[None, :, :, None, None]
    decay = jnp.where(causal, jnp.exp(jnp.minimum(b[:, :, None] - b[:, None, :], 0.0)), 0.0)
    scores = jnp.einsum('bthk,btshk,bshk->bhts', q, decay, k)
    o = (jnp.einsum('bhts,bshv->bthv', scores, v)
         + jnp.einsum('bthk,bhkv->bthv', q * jnp.exp(b), S0))
    b_last = b[:, -1]
    S = S0 * jnp.exp(b_last)[..., None] + jnp.einsum('bshk,bshv->bhkv', k * jnp.exp(b_last[:, None] - b), v)
    return S, o


def hgrn2_scan(S0, q, log_f, k, v):
    Bsz, T = q.shape[:2]
    L = min(T, CHUNK)
    n = T // L
    to_chunks = lambda t: jnp.moveaxis(t.reshape(Bsz, n, L, *t.shape[2:]), 1, 0)
    S, o = lax.scan(lambda S, c: hgrn2_chunk(S, *c), S0, (to_chunks(q), to_chunks(log_f), to_chunks(k), to_chunks(v)))
    return S, jnp.moveaxis(o, 0, 1).reshape(Bsz, T, *o.shape[3:])


def mixer_ab(x, k_hist, v_hist, S0, lb, lam_init, w_in, w_out, a_norm_g, lam_p, b_norm_g, rel_bias):
    Bsz, T, _ = x.shape
    P = k_hist.shape[1]
    h = x @ w_in
    aq, af, ai, ag, bq, bk, bv = jnp.split(h, np.cumsum([A_W] * 4 + [B_W] * 2).tolist(), axis=-1)
    zf = af.astype(jnp.float32)
    log_f = jnp.log(lb + (1.0 - lb) * jax.nn.sigmoid(zf))
    k_in = (1.0 - lb) * jax.nn.sigmoid(-zf)
    ah = lambda t: t.reshape(Bsz, T, A_HEADS, -1)
    S, oa = hgrn2_scan(S0.astype(jnp.float32), ah(aq.astype(jnp.float32)), ah(log_f), ah(k_in),
                       ah(ai.astype(jnp.float32)))
    oa = rms_norm(oa, a_norm_g.reshape(A_HEADS, A_DV)).reshape(Bsz, T, A_W).astype(x.dtype) * jax.nn.silu(ag)
    lp = lam_p.astype(jnp.float32)
    lam = jnp.exp(jnp.sum(lp[0] * lp[1])) - jnp.exp(jnp.sum(lp[2] * lp[3])) + lam_init
    q = bq.reshape(Bsz, T, B_HEADS, 2 * B_DQK)
    k_new = bk.reshape(Bsz, T, B_HEADS, 2 * B_DQK)
    v_new = bv.reshape(Bsz, T, B_HEADS, B_DV)
    k_all = jnp.concatenate([k_hist.astype(x.dtype), k_new], axis=1)
    v_all = jnp.concatenate([v_hist.astype(x.dtype), v_new], axis=1)
    q_pos = P + jnp.arange(T, dtype=jnp.int32)
    k_pos = jnp.arange(P + T, dtype=jnp.int32)
    if T > Q_BLOCK:
        nb = T // Q_BLOCK
        qb = jnp.moveaxis(q.reshape(Bsz, nb, Q_BLOCK, B_HEADS, 2 * B_DQK), 1, 0)
        ob = lax.map(lambda c: diff_attn(c[0], k_all, v_all, c[1], k_pos, lam, rel_bias),
                     (qb, q_pos.reshape(nb, Q_BLOCK)))
        ob = jnp.moveaxis(ob, 0, 1).reshape(Bsz, T, B_HEADS, B_DV)
    else:
        ob = diff_attn(q, k_all, v_all, q_pos, k_pos, lam, rel_bias)
    ob = (rms_norm(ob, b_norm_g) * (1.0 - lam_init)).reshape(Bsz, T, B_W)
    y = jnp.concatenate([oa, ob], axis=-1) @ w_out
    return y, k_new, v_new, S.astype(x.dtype)


def mixer_pool(x, pos0, hist, w_groups, scale):
    Bsz, T, _ = x.shape
    xh = jnp.concatenate([hist.astype(x.dtype), x], axis=1)
    cs = jnp.cumsum(jnp.pad(xh.astype(jnp.float32), ((0, 0), (1, 0), (0, 0))), axis=1)
    pos = pos0 + jnp.arange(T, dtype=jnp.int32)
    outs = []
    for g, w in enumerate(POOL_WINDOWS):
        sl = slice(g * POOL_GROUP, (g + 1) * POOL_GROUP)
        s = cs[:, POOL_HIST + 1:, sl] - cs[:, POOL_HIST + 1 - w:POOL_HIST + 1 - w + T, sl]
        cnt = jnp.minimum(w, pos + 1).astype(jnp.float32)[None, :, None]
        pooled = (s / cnt).astype(x.dtype) - x[..., sl]
        outs.append(pooled @ w_groups[g])
    y = jnp.concatenate(outs, axis=-1) * scale
    return y, xh[:, -POOL_HIST:]


def conv_ffn(x, hist, w_up, conv_w, conv_b, w_down):
    T = x.shape[1]
    a, b = jnp.split(x @ w_up, 2, axis=-1)
    ah = jnp.concatenate([hist.astype(x.dtype), a], axis=1)
    c = conv_b
    for j in range(CONV_W):
        c = c + ah[:, j:j + T] * conv_w[j]
    return (jax.nn.silu(c) * b) @ w_down, ah[:, -(CONV_W - 1):]


def trunk(x, k_hist, v_hist, s_hist, pool_hist, conv_hist, lower_bounds, w_in_ab, w_out_ab, hgrn_norm_g,
          diff_lambda, diff_norm_g, rel_bias, pool_w, pool_scale, ffn_w_up, ffn_conv_w, ffn_conv_b,
          ffn_w_down, ln1_g, ln1_b, ln2_g, ln2_b):
    pos0 = k_hist[0].shape[1]
    ks, vs, ss, ps, cs = [], [], [], [], []
    for l in range(DEPTH):
        if l % 2 == 0:
            m = l // 2
            lam_init = 0.8 - 0.6 * math.exp(-0.3 * l)
            y, kr, vr, S = mixer_ab(x, k_hist[m], v_hist[m], s_hist[m], lower_bounds[m], lam_init,
                                    w_in_ab[m], w_out_ab[m], hgrn_norm_g[m], diff_lambda[m],
                                    diff_norm_g[m], rel_bias)
            ks.append(kr); vs.append(vr); ss.append(S)
        else:
            p = l // 2
            y, ph = mixer_pool(x, pos0, pool_hist[p], pool_w[p], pool_scale[p])
            ps.append(ph)
        x = layer_norm(DN_ALPHA * x + y, ln1_g[l], ln1_b[l])
        f, ch = conv_ffn(x, conv_hist[l], ffn_w_up[l], ffn_conv_w[l], ffn_conv_b[l], ffn_w_down[l])
        cs.append(ch)
        x = layer_norm(DN_ALPHA * x + f, ln2_g[l], ln2_b[l])
    return x, jnp.stack(ks), jnp.stack(vs), jnp.stack(ss), jnp.stack(ps), jnp.stack(cs)


def setup_inputs(seed: int = 0) -> dict:
    key = jax.random.key(seed)
    ks = jax.random.split(key, 24)
    nrm = lambda k, shape, s: jax.random.normal(k, shape, jnp.float32) * s
    return {
        "x_prompt": nrm(ks[0], (BATCH, SEQ, D_MODEL), 1.0),
        "x_sample": nrm(ks[1], (DEC_BATCH, DEC_SEQ, D_MODEL), 1.0),
        "cache_k": nrm(ks[2], (N_MIX_LAYERS, DEC_BATCH, PAST_LEN, B_HEADS, 2 * B_DQK), 1.0),
        "cache_v": nrm(ks[3], (N_MIX_LAYERS, DEC_BATCH, PAST_LEN, B_HEADS, B_DV), 1.0),
        "state_hgrn": nrm(ks[4], (N_MIX_LAYERS, DEC_BATCH, A_HEADS, A_DK, A_DV), 0.5),
        "state_pool": nrm(ks[5], (N_POOL_LAYERS, DEC_BATCH, POOL_HIST, D_MODEL), 1.0),
        "state_ffn_conv": nrm(ks[6], (DEPTH, DEC_BATCH, CONV_W - 1, D_FF), 1.0),
        "w_in_ab": nrm(ks[7], (N_MIX_LAYERS, D_MODEL, IN_COLS), D_MODEL ** -0.5),
        "w_out_ab": nrm(ks[8], (N_MIX_LAYERS, MIX_W, D_MODEL), DN_BETA * MIX_W ** -0.5),
        "lb_logits": nrm(ks[9], (N_MIX_LAYERS, A_W), 0.5),
        "hgrn_norm_g": 1.0 + nrm(ks[10], (N_MIX_LAYERS, A_W), 0.02),
        "diff_lambda": nrm(ks[11], (N_MIX_LAYERS, 4, B_DQK), 0.1),
        "diff_norm_g": 1.0 + nrm(ks[12], (N_MIX_LAYERS, B_DV), 0.02),
        "rel_bias": nrm(ks[13], (NUM_BUCKETS, B_HEADS), 0.2),
        "pool_w": nrm(ks[14], (N_POOL_LAYERS, len(POOL_WINDOWS), POOL_GROUP, POOL_GROUP), DN_BETA * POOL_GROUP ** -0.5),
        "pool_scale": 1.0 + nrm(ks[15], (N_POOL_LAYERS, D_MODEL), 0.02),
        "ffn_w_up": nrm(ks[16], (DEPTH, D_MODEL, 2 * D_FF), D_MODEL ** -0.5),
        "ffn_conv_w": nrm(ks[17], (DEPTH, CONV_W, D_FF), CONV_W ** -0.5),
        "ffn_conv_b": nrm(ks[18], (DEPTH, D_FF), 0.01),
        "ffn_w_down": nrm(ks[19], (DEPTH, D_FF, D_MODEL), DN_BETA * D_FF ** -0.5),
        "ln1_g": 1.0 + nrm(ks[20], (DEPTH, D_MODEL), 0.02),
        "ln1_b": nrm(ks[21], (DEPTH, D_MODEL), 0.01),
        "ln2_g": 1.0 + nrm(ks[22], (DEPTH, D_MODEL), 0.02),
        "ln2_b": nrm(ks[23], (DEPTH, D_MODEL), 0.01),
    }


def reference(x_prompt, x_sample, cache_k, cache_v, state_hgrn, state_pool, state_ffn_conv, w_in_ab, w_out_ab,
              lb_logits, hgrn_norm_g, diff_lambda, diff_norm_g, rel_bias, pool_w, pool_scale, ffn_w_up,
              ffn_conv_w, ffn_conv_b, ffn_w_down, ln1_g, ln1_b, ln2_g, ln2_b):
    sm = jax.nn.softmax(lb_logits.astype(jnp.float32), axis=0)
    lower_bounds = jnp.cumsum(sm, axis=0) - sm[0]
    weights = (lower_bounds, w_in_ab, w_out_ab, hgrn_norm_g, diff_lambda, diff_norm_g, rel_bias, pool_w,
               pool_scale, ffn_w_up, ffn_conv_w, ffn_conv_b, ffn_w_down, ln1_g, ln1_b, ln2_g, ln2_b)
    Bp, dt = x_prompt.shape[0], x_prompt.dtype
    y_prompt, k_p, v_p, s_p, pool_p, conv_p = trunk(
        x_prompt,
        [jnp.zeros((Bp, 0, B_HEADS, 2 * B_DQK), dt)] * N_MIX_LAYERS,
        [jnp.zeros((Bp, 0, B_HEADS, B_DV), dt)] * N_MIX_LAYERS,
        [jnp.zeros((Bp, A_HEADS, A_DK, A_DV), dt)] * N_MIX_LAYERS,
        [jnp.zeros((Bp, POOL_HIST, D_MODEL), dt)] * N_POOL_LAYERS,
        [jnp.zeros((Bp, CONV_W - 1, D_FF), dt)] * DEPTH,
        *weights)
    y_sample, k_s, v_s, s_s, pool_s, conv_s = trunk(
        x_sample, cache_k, cache_v, state_hgrn, state_pool, state_ffn_conv, *weights)
    return (y_prompt, y_sample, k_p, v_p, s_p, pool_p, conv_p, k_s, v_s, s_s, pool_s, conv_s)
```

```python
import functools
import math

import numpy as np
import jax
import jax.numpy as jnp
from jax import lax
from jax.experimental import pallas as pl
from jax.experimental.pallas import tpu as pltpu

F32 = jnp.float32
BF16 = jnp.bfloat16

D_MODEL = 1024
DEPTH = 4
CHUNK = 64
N_MIX = (DEPTH + 1) // 2
HEADS = 4
HEAD_W = 128
A_W = HEADS * HEAD_W
B_DQK = 64
POOL_WINDOWS = (2, 4, 8, 16)
POOL_GROUP = D_MODEL // len(POOL_WINDOWS)
POOL_HIST = max(POOL_WINDOWS) - 1
D_FF = 256 * math.ceil(8 * D_MODEL / 3 / 256)
CONV_W = 3
NUM_BUCKETS = 32
MAX_DISTANCE = 128
DN_ALPHA = (2 * DEPTH) ** 0.25
LN_EPS = 1e-5
RMS_EPS = 1e-6
MASK_NEG = -1e30

VMEM_LIMIT_BYTES = 56 * 1024 * 1024
ATT_TILE = 256
ROW_TILE = 512
FF_CHUNK = 256
N_PROJ = 7


def _nt(a, b):
    return lax.dot_general(a, b, (((1,), (1,)), ((), ())), preferred_element_type=F32)


def _tn(a, b):
    return lax.dot_general(a, b, (((0,), (0,)), ((), ())), preferred_element_type=F32)


def _mm(a, b):
    return jnp.dot(a, b, preferred_element_type=F32)


def _layer_norm(y, g, b):
    mu = jnp.mean(y, axis=-1, keepdims=True)
    yc = y - mu
    var = jnp.mean(yc * yc, axis=-1, keepdims=True)
    return yc * lax.rsqrt(var + LN_EPS) * g + b


def _rms_norm(y, g):
    return y * lax.rsqrt(jnp.mean(y * y, axis=-1, keepdims=True) + RMS_EPS) * g


def _compiler_params(semantics):
    return pltpu.CompilerParams(dimension_semantics=semantics, vmem_limit_bytes=VMEM_LIMIT_BYTES)


def _rel_bucket(rel):
    half = NUM_BUCKETS // 2
    max_exact = half // 2
    base = jnp.where(rel > 0, half, 0)
    n = jnp.abs(rel)
    nf = jnp.maximum(n, 1).astype(jnp.float32)
    large = max_exact + (jnp.log(nf / max_exact) / math.log(MAX_DISTANCE / max_exact)
                         * (half - max_exact)).astype(jnp.int32)
    large = jnp.minimum(large, half - 1)
    return base + jnp.where(n < max_exact, n, large)


def _bias_tile_body(far_bucket, bucket_ref, rb_ref, out_ref):
    h = pl.program_id(0)
    bucket = bucket_ref[...]
    acc = jnp.zeros(bucket.shape, F32)
    for b in range(NUM_BUCKETS):
        acc = jnp.where(bucket == b, rb_ref[b, h], acc)
    out_ref[...] = jnp.where(bucket < 0, MASK_NEG, acc - rb_ref[far_bucket, h])


def _bias_tiles(rel_bias, q_pos_tiles, k_pos_tiles):
    buckets = []
    for qp, kp in zip(q_pos_tiles, k_pos_tiles):
        qp = jnp.asarray(qp, jnp.int32)
        kp = jnp.asarray(kp, jnp.int32)
        visible = (kp[None, :] // CHUNK) <= (qp[:, None] // CHUNK)
        buckets.append(jnp.where(visible & (kp[None, :] >= 0), _rel_bucket(kp[None, :] - qp[:, None]), -1))
    bucket = jnp.stack(buckets).astype(jnp.int32)
    n, r, c = bucket.shape
    far_bucket = NUM_BUCKETS // 2 - 1
    return pl.pallas_call(
        functools.partial(_bias_tile_body, far_bucket),
        grid=(HEADS, n),
        in_specs=[pl.BlockSpec((None, r, c), lambda h, i: (i, 0, 0)),
                  pl.BlockSpec(memory_space=pltpu.SMEM)],
        out_specs=pl.BlockSpec((None, None, r, c), lambda h, i: (h, i, 0, 0)),
        out_shape=jax.ShapeDtypeStruct((HEADS, n, r, c), F32),
        name="bias_tiles",
    )(bucket, rel_bias.astype(F32))


def _hgrn_constants(L):
    levels = []
    w = L // 2
    while w >= 1:
        levels.append(w)
        w //= 2
    t = np.arange(L)[:, None]
    u = np.arange(L)[None, :]
    blocks = [(u <= t)]
    for w in levels:
        start = (t // w) * w
        blocks.append((u > start) & (u <= t))
        nxt = np.minimum((t // w + 1) * w, L - 1)
        blocks.append((u > t) & (u <= nxt))
    g = np.concatenate(blocks, axis=0).astype(np.float32)
    g3 = np.concatenate([g, g, g], axis=1)
    level = np.full((L, L), -1, np.int32)
    s = u
    for i, w in enumerate(levels):
        hit = (t // (2 * w) == s // (2 * w)) & ((t // w) % 2 == 1) & ((s // w) % 2 == 0)
        level[hit] = i
    level[np.arange(L), np.arange(L)] = len(levels)
    return jnp.asarray(g3, BF16), jnp.asarray(level, jnp.int32), len(levels)


def _softmax_step(carry, s, vt):
    m, l, acc = carry
    m_new = jnp.maximum(m, jnp.max(s, axis=-1, keepdims=True))
    p = jnp.exp(s - m_new)
    alpha = jnp.exp(m - m_new)
    l = alpha * l + jnp.sum(p, axis=-1, keepdims=True)
    acc = alpha * acc + _mm(p.astype(BF16), vt)
    return m_new, l, acc


def _mixer_body(cfg, *refs):
    T, P, L, m_idx, lam_init, n_levels = cfg
    has_hist = P > 0
    (x_ref, w_ref, lbl_ref, ag_ref, bg_ref, dl_ref, bias_ref, g_ref, lev_ref), refs = refs[:9], refs[9:]
    if has_hist:
        (hk_ref, hv_ref, s0_ref), refs = refs[:3], refs[3:]
    (oa_ref, ob_ref, k_ref, v_ref, s_ref,
     xb_ref, proj_ref, st_ref, q1_ref, q2_ref, kb_ref, vb_ref) = refs
    h = pl.program_id(1)

    @pl.when(h == 0)
    def _():
        xb_ref[...] = x_ref[...].astype(BF16)

    rt = min(T, ROW_TILE)

    def proj_tile(i, carry):
        rows = pl.ds(pl.multiple_of(i * rt, rt), rt)
        proj_ref[rows, :] = _mm(xb_ref[rows, :], w_ref[...])
        return carry

    lax.fori_loop(0, T // rt, proj_tile, 0)

    lg = lbl_ref[...]
    e = jnp.exp(lg - jnp.max(lg, axis=0, keepdims=True))
    sm = e / jnp.sum(e, axis=0, keepdims=True)
    lb = jnp.sum(sm[0:m_idx + 1], axis=0, keepdims=True) - sm[0:1]
    a_gain = ag_ref[...]

    if has_hist:
        st_ref[...] = s0_ref[...].T
    else:
        st_ref[...] = jnp.zeros((HEAD_W, HEAD_W), F32)
    level = lev_ref[...]

    def chunk(c, carry):
        rows = pl.ds(pl.multiple_of(c * L, L), L)
        q = proj_ref[rows, 0 * HEAD_W:1 * HEAD_W]
        zf = proj_ref[rows, 1 * HEAD_W:2 * HEAD_W]
        vi = proj_ref[rows, 2 * HEAD_W:3 * HEAD_W].astype(BF16)
        gate = proj_ref[rows, 3 * HEAD_W:4 * HEAD_W]
        log_f = jnp.log(lb + (1.0 - lb) * jax.nn.sigmoid(zf))
        k_in = (1.0 - lb) * jax.nn.sigmoid(-zf)
        hi = log_f.astype(BF16)
        r1 = log_f - hi.astype(F32)
        mid = r1.astype(BF16)
        lo = (r1 - mid.astype(F32)).astype(BF16)
        ex = _mm(g_ref[...], jnp.concatenate([hi, mid, lo], axis=0))
        b = ex[0:L]
        scores = jnp.zeros((L, L), F32)
        for i in range(n_levels):
            qw = (q * jnp.exp(ex[(1 + 2 * i) * L:(2 + 2 * i) * L])).astype(BF16)
            kw = (k_in * jnp.exp(ex[(2 + 2 * i) * L:(3 + 2 * i) * L])).astype(BF16)
            scores = jnp.where(level == i, _nt(qw, kw), scores)
        scores = jnp.where(level == n_levels, _nt(q.astype(BF16), k_in.astype(BF16)), scores)
        st = st_ref[...]
        o = _mm(scores.astype(BF16), vi) + _nt((q * jnp.exp(b)).astype(BF16), st.astype(BF16))
        b_last = b[L - 1:L]
        kd = (k_in * jnp.exp(b_last - b)).astype(BF16)
        st_ref[...] = st * jnp.exp(b_last) + _tn(vi, kd)
        oa_ref[rows, :] = (_rms_norm(o, a_gain) * (gate * jax.nn.sigmoid(gate))).astype(BF16)
        return carry

    lax.fori_loop(0, T // L, chunk, 0)
    s_ref[...] = st_ref[...].T

    lane = lax.broadcasted_iota(jnp.int32, (1, HEAD_W), 1)
    lp = dl_ref[...]
    lam = (jnp.exp(jnp.sum(lp[0:1] * lp[1:2], axis=-1, keepdims=True))
           - jnp.exp(jnp.sum(lp[2:3] * lp[3:4], axis=-1, keepdims=True)) + lam_init)
    qs = proj_ref[:, 4 * HEAD_W:5 * HEAD_W] * (B_DQK ** -0.5)
    q1_ref[...] = jnp.where(lane < B_DQK, qs, 0.0).astype(BF16)
    q2_ref[...] = jnp.where(lane >= B_DQK, qs, 0.0).astype(BF16)
    k_new = proj_ref[:, 5 * HEAD_W:6 * HEAD_W]
    v_new = proj_ref[:, 6 * HEAD_W:7 * HEAD_W]
    k_ref[...] = k_new
    v_ref[...] = v_new
    b_gain = bg_ref[...] * (1.0 - lam_init)

    def finish(c1, c2):
        o = c1[2] / c1[1] - lam * (c2[2] / c2[1])
        return (_rms_norm(o, 1.0) * b_gain).astype(BF16)

    def init(rows):
        return (jnp.full((rows, 1), MASK_NEG, F32), jnp.zeros((rows, 1), F32), jnp.zeros((rows, HEAD_W), F32))

    if not has_hist:
        kb_ref[...] = k_new.astype(BF16)
        vb_ref[...] = v_new.astype(BF16)
        tq = min(T, ATT_TILE)

        def q_tile(i, carry):
            rows = pl.ds(pl.multiple_of(i * tq, tq), tq)
            q1 = q1_ref[rows, :]
            q2 = q2_ref[rows, :]

            def k_tile(j, cc):
                krows = pl.ds(pl.multiple_of(j * tq, tq), tq)
                kt = kb_ref[krows, :]
                vt = vb_ref[krows, :]
                bias = bias_ref[jnp.minimum(i - j, 2)]
                return (_softmax_step(cc[0], _nt(q1, kt) + bias, vt),
                        _softmax_step(cc[1], _nt(q2, kt) + bias, vt))

            c1, c2 = lax.fori_loop(0, i + 1, k_tile, (init(tq), init(tq)))
            ob_ref[rows, :] = finish(c1, c2)
            return carry

        lax.fori_loop(0, T // tq, q_tile, 0)
    else:
        kb_ref[...] = jnp.zeros(kb_ref.shape, BF16)
        vb_ref[...] = jnp.zeros(vb_ref.shape, BF16)
        kb_ref[0:T, :] = k_new.astype(BF16)
        vb_ref[0:T, :] = v_new.astype(BF16)
        q1 = q1_ref[...]
        q2 = q2_ref[...]
        tk = ATT_TILE
        n_hist = P // tk

        def hist_step(cc, j, bias):
            krows = pl.ds(pl.multiple_of(j * tk, tk), tk)
            kt = hk_ref[krows, :].astype(BF16)
            vt = hv_ref[krows, :].astype(BF16)
            s1 = _nt(q1, kt)
            s2 = _nt(q2, kt)
            if bias is not None:
                s1 = s1 + bias
                s2 = s2 + bias
            return _softmax_step(cc[0], s1, vt), _softmax_step(cc[1], s2, vt)

        cc = lax.fori_loop(0, n_hist - 1, lambda j, cc: hist_step(cc, j, None), (init(T), init(T)))
        cc = hist_step(cc, n_hist - 1, bias_ref[0])
        bias_new = bias_ref[1][:, 0:HEAD_W]
        kt = kb_ref[...]
        vt = vb_ref[...]
        c1 = _softmax_step(cc[0], _nt(q1, kt) + bias_new, vt)
        c2 = _softmax_step(cc[1], _nt(q2, kt) + bias_new, vt)
        ob_ref[...] = finish(c1, c2)


def _mixer(x, w7, lb_logits, a_gain, b_gain, dlam, bias, hist, m_idx, lam_init):
    B, T, D = x.shape
    L = min(T, CHUNK)
    gmat, level, n_levels = _hgrn_constants(L)
    P = 0 if hist is None else hist[0].shape[1]
    assert T % L == 0 and T % min(T, ROW_TILE) == 0 and T % min(T, ATT_TILE) == 0
    assert ATT_TILE > MAX_DISTANCE and (T <= ATT_TILE if hist is not None else True)
    assert P % ATT_TILE == 0
    cfg = (T, P, L, m_idx, lam_init, n_levels)
    wcols = N_PROJ * HEAD_W
    in_specs = [
        pl.BlockSpec((None, T, D), lambda b, h: (b, 0, 0)),
        pl.BlockSpec((None, D, wcols), lambda b, h: (h, 0, 0)),
        pl.BlockSpec((None, N_MIX, HEAD_W), lambda b, h: (h, 0, 0)),
        pl.BlockSpec((None, 1, HEAD_W), lambda b, h: (h, 0, 0)),
        pl.BlockSpec((1, HEAD_W), lambda b, h: (0, 0)),
        pl.BlockSpec(dlam.shape, lambda b, h: (0, 0)),
        pl.BlockSpec((None,) + bias.shape[1:], lambda b, h: (h, 0, 0, 0)),
        pl.BlockSpec(gmat.shape, lambda b, h: (0, 0)),
        pl.BlockSpec(level.shape, lambda b, h: (0, 0)),
    ]
    args = [x, w7, lb_logits, a_gain, b_gain, dlam, bias, gmat, level]
    if hist is not None:
        hk, hv, s0 = hist
        in_specs += [
            pl.BlockSpec((None, P, HEAD_W), lambda b, h: (b, 0, h)),
            pl.BlockSpec((None, P, HEAD_W), lambda b, h: (b, 0, h)),
            pl.BlockSpec((None, None, HEAD_W, HEAD_W), lambda b, h: (b, h, 0, 0)),
        ]
        args += [hk, hv, s0]
    col_spec = pl.BlockSpec((None, T, HEAD_W), lambda b, h: (b, 0, h))
    key_rows = T if hist is None else HEAD_W
    return pl.pallas_call(
        functools.partial(_mixer_body, cfg),
        grid=(B, HEADS),
        in_specs=in_specs,
        out_specs=[col_spec, col_spec, col_spec, col_spec,
                   pl.BlockSpec((None, None, HEAD_W, HEAD_W), lambda b, h: (b, h, 0, 0))],
        out_shape=[jax.ShapeDtypeStruct((B, T, A_W), BF16), jax.ShapeDtypeStruct((B, T, A_W), BF16),
                   jax.ShapeDtypeStruct((B, T, A_W), F32), jax.ShapeDtypeStruct((B, T, A_W), F32),
                   jax.ShapeDtypeStruct((B, HEADS, HEAD_W, HEAD_W), F32)],
        scratch_shapes=[pltpu.VMEM((T, D), BF16), pltpu.VMEM((T, wcols), F32),
                        pltpu.VMEM((HEAD_W, HEAD_W), F32),
                        pltpu.VMEM((T, HEAD_W), BF16), pltpu.VMEM((T, HEAD_W), BF16),
                        pltpu.VMEM((key_rows, HEAD_W), BF16), pltpu.VMEM((key_rows, HEAD_W), BF16)],
        compiler_params=_compiler_params(("arbitrary", "arbitrary")),
        name="mixer_ab",
    )(*args)


def _out_proj_body(x_ref, oa_ref, ob_ref, w_ref, g_ref, b_ref, out_ref):
    y = _mm(oa_ref[...], w_ref[0:A_W, :]) + _mm(ob_ref[...], w_ref[A_W:2 * A_W, :])
    out_ref[...] = _layer_norm(DN_ALPHA * x_ref[...] + y, g_ref[...], b_ref[...])


def _out_proj_ln(x, oa, ob, w_out, g, b):
    N, D = x.shape
    tm = min(N, ROW_TILE)
    row = lambda w: pl.BlockSpec((tm, w), lambda i: (i, 0))
    full = lambda a: pl.BlockSpec(a.shape, lambda i: (0,) * a.ndim)
    return pl.pallas_call(
        _out_proj_body,
        grid=(N // tm,),
        in_specs=[row(D), row(A_W), row(A_W), full(w_out), full(g), full(b)],
        out_specs=row(D),
        out_shape=jax.ShapeDtypeStruct((N, D), F32),
        compiler_params=_compiler_params(("arbitrary",)),
        name="out_proj_ln",
    )(x, oa, ob, w_out, g, b)


def _pool_body(cfg, x_ref, prev_ref, hist_ref, pw_ref, sc_ref, g_ref, b_ref, out_ref, ph_ref, xs_ref):
    tm, pos0 = cfg
    H = POOL_HIST + 1
    i = pl.program_id(1)
    x = x_ref[...]
    xs_ref[H:H + tm, :] = x

    @pl.when(i == 0)
    def _():
        xs_ref[0:1, :] = jnp.zeros((1, D_MODEL), F32)
        xs_ref[1:H, :] = hist_ref[...]

    @pl.when(i > 0)
    def _():
        xs_ref[0:H, :] = prev_ref[...]

    pos = pos0 + i * tm + lax.broadcasted_iota(jnp.int32, (tm, 1), 0)
    ys = []
    for g, w in enumerate(POOL_WINDOWS):
        cols = slice(g * POOL_GROUP, (g + 1) * POOL_GROUP)
        s = x[:, cols]
        for k in range(1, w):
            s = s + xs_ref[H - k:H - k + tm, cols]
        cnt = jnp.minimum(w, pos + 1).astype(F32)
        pooled = s / cnt - x[:, cols]
        ys.append(_mm(pooled.astype(BF16), pw_ref[g]))
    y = jnp.concatenate(ys, axis=-1) * sc_ref[...]
    out_ref[...] = _layer_norm(DN_ALPHA * x + y, g_ref[...], b_ref[...])
    ph_ref[...] = xs_ref[tm + 1:tm + H, :]


def _pool_ln(x, hist, pool_w, scale, g, b, pos0):
    B, T, D = x.shape
    H = POOL_HIST + 1
    tm = min(T, ROW_TILE)
    per = tm // H
    full = lambda a: pl.BlockSpec(a.shape, lambda bi, i: (0,) * a.ndim)
    return pl.pallas_call(
        functools.partial(_pool_body, (tm, pos0)),
        grid=(B, T // tm),
        in_specs=[pl.BlockSpec((None, tm, D), lambda bi, i: (bi, i, 0)),
                  pl.BlockSpec((None, H, D), lambda bi, i: (bi, jnp.maximum(i * per - 1, 0), 0)),
                  pl.BlockSpec((None, POOL_HIST, D), lambda bi, i: (bi, 0, 0)),
                  full(pool_w), full(scale), full(g), full(b)],
        out_specs=[pl.BlockSpec((None, tm, D), lambda bi, i: (bi, i, 0)),
                   pl.BlockSpec((None, POOL_HIST, D), lambda bi, i: (bi, 0, 0))],
        out_shape=[jax.ShapeDtypeStruct((B, T, D), F32), jax.ShapeDtypeStruct((B, POOL_HIST, D), F32)],
        scratch_shapes=[pltpu.VMEM((H + tm, D), F32)],
        compiler_params=_compiler_params(("arbitrary", "arbitrary")),
        name="pool_ln",
    )(x, x, hist, pool_w, scale, g, b)


def _ffn_body(cfg, x_ref, hist_ref, wup_ref, cw_ref, cb_ref, wdn_ref, g_ref, b_ref,
              out_ref, cs_ref, abuf_ref, carry_ref, gbuf_ref):
    tm, sh = cfg
    halo = (CONV_W - 1) * sh
    pad = -(-halo // 8) * 8
    n_chunks = D_FF // FF_CHUNK
    i = pl.program_id(1)
    x = x_ref[...]
    xb = x.astype(BF16)
    for c in range(n_chunks):
        ab = _mm(xb, wup_ref[c])
        a = ab[:, 0:FF_CHUNK]
        abuf_ref[pad:pad + tm, :] = a

        @pl.when(i == 0)
        def _():
            abuf_ref[pad - halo:pad, :] = hist_ref[c]

        @pl.when(i > 0)
        def _():
            abuf_ref[pad - halo:pad, :] = carry_ref[c]

        conv = cb_ref[c]
        for j in range(CONV_W - 1):
            lo = pad - halo + j * sh
            conv = conv + abuf_ref[lo:lo + tm, :] * cw_ref[c, j:j + 1, :]
        conv = conv + a * cw_ref[c, CONV_W - 1:CONV_W, :]
        tail = abuf_ref[pad + tm - halo:pad + tm, :]
        carry_ref[c] = tail
        cs_ref[c] = tail
        gbuf_ref[:, c * FF_CHUNK:(c + 1) * FF_CHUNK] = (
            conv * jax.nn.sigmoid(conv) * ab[:, FF_CHUNK:2 * FF_CHUNK]).astype(BF16)
    f = _mm(gbuf_ref[...], wdn_ref[...])
    out_ref[...] = _layer_norm(DN_ALPHA * x + f, g_ref[...], b_ref[...])


def _conv_ffn_ln(x, hist, w_up, conv_w, conv_b, w_down, g, b, sh):
    B, T, D = x.shape
    halo = (CONV_W - 1) * sh
    tm = min(T, ROW_TILE)
    n_chunks = D_FF // FF_CHUNK
    pad = -(-halo // 8) * 8
    hist_c = hist.reshape(B, halo, n_chunks, FF_CHUNK).transpose(0, 2, 1, 3)
    full = lambda a: pl.BlockSpec(a.shape, lambda bi, i: (0,) * a.ndim)
    out, cs = pl.pallas_call(
        functools.partial(_ffn_body, (tm, sh)),
        grid=(B, T // tm),
        in_specs=[pl.BlockSpec((None, tm, D), lambda bi, i: (bi, i, 0)),
                  pl.BlockSpec((None, n_chunks, halo, FF_CHUNK), lambda bi, i: (bi, 0, 0, 0)),
                  full(w_up), full(conv_w), full(conv_b), full(w_down), full(g), full(b)],
        out_specs=[pl.BlockSpec((None, tm, D), lambda bi, i: (bi, i, 0)),
                   pl.BlockSpec((None, n_chunks, halo, FF_CHUNK), lambda bi, i: (bi, 0, 0, 0))],
        out_shape=[jax.ShapeDtypeStruct((B, T, D), F32),
                   jax.ShapeDtypeStruct((B, n_chunks, halo, FF_CHUNK), F32)],
        scratch_shapes=[pltpu.VMEM((pad + tm, FF_CHUNK), F32),
                        pltpu.VMEM((n_chunks, halo, FF_CHUNK), F32),
                        pltpu.VMEM((tm, D_FF), BF16)],
        compiler_params=_compiler_params(("arbitrary", "arbitrary")),
        name="conv_ffn_ln",
    )(x, hist_c, w_up, conv_w, conv_b, w_down, g, b)
    return out, cs.transpose(0, 2, 1, 3).reshape(B, halo, D_FF)


def _prep_weights(w_in_ab, w_out_ab, lb_logits, hgrn_norm_g, diff_norm_g, pool_w, pool_scale, ffn_w_up,
                  ffn_conv_w, ffn_conv_b, ffn_w_down, ln1_g, ln1_b, ln2_g, ln2_b):
    n_chunks = D_FF // FF_CHUNK
    w = {}
    w["w7"] = (w_in_ab.astype(BF16).reshape(N_MIX, D_MODEL, N_PROJ, HEADS, HEAD_W)
               .transpose(0, 3, 1, 2, 4).reshape(N_MIX, HEADS, D_MODEL, N_PROJ * HEAD_W))
    w["w_out"] = w_out_ab.astype(BF16)
    w["lb_logits"] = lb_logits.astype(F32).reshape(N_MIX, HEADS, HEAD_W).transpose(1, 0, 2)
    w["a_gain"] = hgrn_norm_g.astype(F32).reshape(N_MIX, HEADS, 1, HEAD_W)
    w["b_gain"] = diff_norm_g.astype(F32).reshape(N_MIX, 1, HEAD_W)
    w["pool_w"] = pool_w.astype(BF16)
    w["pool_scale"] = pool_scale.astype(F32).reshape(-1, 1, D_MODEL)
    up = ffn_w_up.astype(BF16).reshape(DEPTH, D_MODEL, 2, n_chunks, FF_CHUNK)
    w["w_up"] = up.transpose(0, 3, 1, 2, 4).reshape(DEPTH, n_chunks, D_MODEL, 2 * FF_CHUNK)
    w["conv_w"] = ffn_conv_w.astype(F32).reshape(DEPTH, CONV_W, n_chunks, FF_CHUNK).transpose(0, 2, 1, 3)
    w["conv_b"] = ffn_conv_b.astype(F32).reshape(DEPTH, n_chunks, 1, FF_CHUNK)
    w["w_down"] = ffn_w_down.astype(BF16)
    for name, v in (("ln1_g", ln1_g), ("ln1_b", ln1_b), ("ln2_g", ln2_g), ("ln2_b", ln2_b)):
        w[name] = v.astype(F32).reshape(DEPTH, 1, D_MODEL)
    return w


def _attention_bias(rel_bias, T, P):
    if P == 0:
        tq = min(T, ATT_TILE)
        r = np.arange(tq)
        q_tiles = [2 * tq + r] * 3
        k_tiles = [2 * tq + r, tq + r, r]
    else:
        r = P + np.arange(T)
        new = np.where(np.arange(ATT_TILE) < T, P + np.arange(ATT_TILE), -1)
        q_tiles = [r, r]
        k_tiles = [P - ATT_TILE + np.arange(ATT_TILE), new]
    return _bias_tiles(rel_bias, q_tiles, k_tiles)


def _trunk(x, k_hist, v_hist, s_hist, pool_hist, conv_hist, w, diff_lambda, rel_bias, time_major_ffn):
    B, T, D = x.shape
    P = 0 if k_hist is None else k_hist.shape[2]
    bias = _attention_bias(rel_bias, T, P)
    ks, vs, ss, ps, cs = [], [], [], [], []
    for l in range(DEPTH):
        if l % 2 == 0:
            m = l // 2
            lam_init = 0.8 - 0.6 * math.exp(-0.3 * l)
            hist = None
            if k_hist is not None:
                hist = (k_hist[m].reshape(B, P, A_W), v_hist[m].reshape(B, P, A_W), s_hist[m])
            oa, ob, k_new, v_new, s_new = _mixer(
                x, w["w7"][m], w["lb_logits"], w["a_gain"][m], w["b_gain"][m], diff_lambda[m].astype(F32),
                bias, hist, m, lam_init)
            ks.append(k_new.reshape(B, T, HEADS, HEAD_W))
            vs.append(v_new.reshape(B, T, HEADS, HEAD_W))
            ss.append(s_new)
            x = _out_proj_ln(x.reshape(B * T, D), oa.reshape(B * T, A_W), ob.reshape(B * T, A_W),
                             w["w_out"][m], w["ln1_g"][l], w["ln1_b"][l]).reshape(B, T, D)
        else:
            p = l // 2
            x, ph = _pool_ln(x, pool_hist[p], w["pool_w"][p], w["pool_scale"][p], w["ln1_g"][l], w["ln1_b"][l], P)
            ps.append(ph)
        ffn_w = (w["w_up"][l], w["conv_w"][l], w["conv_b"][l], w["w_down"][l], w["ln2_g"][l], w["ln2_b"][l])
        if time_major_ffn:
            xt = x.transpose(1, 0, 2).reshape(1, T * B, D)
            ht = conv_hist[l].transpose(1, 0, 2).reshape(1, (CONV_W - 1) * B, D_FF)
            xt, ch = _conv_ffn_ln(xt, ht, *ffn_w, sh=B)
            x = xt.reshape(T, B, D).transpose(1, 0, 2)
            ch = ch.reshape(CONV_W - 1, B, D_FF).transpose(1, 0, 2)
        else:
            x, ch = _conv_ffn_ln(x, conv_hist[l], *ffn_w, sh=1)
        cs.append(ch)
    return x, jnp.stack(ks), jnp.stack(vs), jnp.stack(ss), jnp.stack(ps), jnp.stack(cs)


def kernel(x_prompt, x_sample, cache_k, cache_v, state_hgrn, state_pool, state_ffn_conv, w_in_ab, w_out_ab,
           lb_logits, hgrn_norm_g, diff_lambda, diff_norm_g, rel_bias, pool_w, pool_scale, ffn_w_up,
           ffn_conv_w, ffn_conv_b, ffn_w_down, ln1_g, ln1_b, ln2_g, ln2_b):
    w = _prep_weights(w_in_ab, w_out_ab, lb_logits, hgrn_norm_g, diff_norm_g, pool_w, pool_scale, ffn_w_up,
                      ffn_conv_w, ffn_conv_b, ffn_w_down, ln1_g, ln1_b, ln2_g, ln2_b)
    Bp, Tp, _ = x_prompt.shape
    dt = x_prompt.dtype
    y_p, k_p, v_p, s_p, pool_p, conv_p = _trunk(
        x_prompt, None, None, None,
        jnp.zeros((DEPTH // 2, Bp, POOL_HIST, D_MODEL), dt),
        jnp.zeros((DEPTH, Bp, CONV_W - 1, D_FF), dt),
        w, diff_lambda, rel_bias, time_major_ffn=False)
    y_s, k_s, v_s, s_s, pool_s, conv_s = _trunk(
        x_sample, cache_k, cache_v, state_hgrn, state_pool, state_ffn_conv,
        w, diff_lambda, rel_bias, time_major_ffn=True)
    return (y_p, y_s, k_p, v_p, s_p, pool_p, conv_p, k_s, v_s, s_s, pool_s, conv_s)
```

```python
import functools
import math

import numpy as np
import jax
import jax.numpy as jnp
from jax import lax
from jax.experimental import pallas as pl
from jax.experimental.pallas import tpu as pltpu

F32 = jnp.float32
BF16 = jnp.bfloat16

D_MODEL = 1024
DEPTH = 4
CHUNK = 64
N_MIX = (DEPTH + 1) // 2
HEADS = 4
HEAD_W = 128
A_W = HEADS * HEAD_W
B_DQK = 64
POOL_WINDOWS = (2, 4, 8, 16)
POOL_GROUP = D_MODEL // len(POOL_WINDOWS)
POOL_HIST = max(POOL_WINDOWS) - 1
D_FF = 256 * math.ceil(8 * D_MODEL / 3 / 256)
CONV_W = 3
NUM_BUCKETS = 32
MAX_DISTANCE = 128
DN_ALPHA = (2 * DEPTH) ** 0.25
LN_EPS = 1e-5
RMS_EPS = 1e-6
MASK_NEG = -1e30

VMEM_LIMIT_BYTES = 56 * 1024 * 1024
ATT_TILE = 256
ROW_TILE = 512
FF_CHUNK = 256
N_PROJ = 7
HGRN_SUB = 4


def _nt(a, b):
    return lax.dot_general(a, b, (((1,), (1,)), ((), ())), preferred_element_type=F32)


def _tn(a, b):
    return lax.dot_general(a, b, (((0,), (0,)), ((), ())), preferred_element_type=F32)


def _mm(a, b):
    return jnp.dot(a, b, preferred_element_type=F32)


def _layer_norm(y, g, b):
    mu = jnp.mean(y, axis=-1, keepdims=True)
    yc = y - mu
    var = jnp.mean(yc * yc, axis=-1, keepdims=True)
    return yc * lax.rsqrt(var + LN_EPS) * g + b


def _rms_norm(y, g):
    return y * lax.rsqrt(jnp.mean(y * y, axis=-1, keepdims=True) + RMS_EPS) * g


def _compiler_params(semantics):
    return pltpu.CompilerParams(dimension_semantics=semantics, vmem_limit_bytes=VMEM_LIMIT_BYTES)


def _rel_bucket(rel):
    half = NUM_BUCKETS // 2
    max_exact = half // 2
    base = jnp.where(rel > 0, half, 0)
    n = jnp.abs(rel)
    nf = jnp.maximum(n, 1).astype(jnp.float32)
    large = max_exact + (jnp.log(nf / max_exact) / math.log(MAX_DISTANCE / max_exact)
                         * (half - max_exact)).astype(jnp.int32)
    large = jnp.minimum(large, half - 1)
    return base + jnp.where(n < max_exact, n, large)


def _bias_tile_body(far_bucket, bucket_ref, rb_ref, out_ref):
    h = pl.program_id(0)
    bucket = bucket_ref[...]
    acc = jnp.zeros(bucket.shape, F32)
    for b in range(NUM_BUCKETS):
        acc = jnp.where(bucket == b, rb_ref[b, h], acc)
    out_ref[...] = jnp.where(bucket < 0, MASK_NEG, acc - rb_ref[far_bucket, h])


def _bias_tiles(rel_bias, q_pos_tiles, k_pos_tiles):
    buckets = []
    for qp, kp in zip(q_pos_tiles, k_pos_tiles):
        qp = jnp.asarray(qp, jnp.int32)
        kp = jnp.asarray(kp, jnp.int32)
        visible = (kp[None, :] // CHUNK) <= (qp[:, None] // CHUNK)
        buckets.append(jnp.where(visible & (kp[None, :] >= 0), _rel_bucket(kp[None, :] - qp[:, None]), -1))
    bucket = jnp.stack(buckets).astype(jnp.int32)
    n, r, c = bucket.shape
    far_bucket = NUM_BUCKETS // 2 - 1
    return pl.pallas_call(
        functools.partial(_bias_tile_body, far_bucket),
        grid=(HEADS, n),
        in_specs=[pl.BlockSpec((None, r, c), lambda h, i: (i, 0, 0)),
                  pl.BlockSpec(memory_space=pltpu.SMEM)],
        out_specs=pl.BlockSpec((None, None, r, c), lambda h, i: (h, i, 0, 0)),
        out_shape=jax.ShapeDtypeStruct((HEADS, n, r, c), F32),
        name="bias_tiles",
    )(bucket, rel_bias.astype(F32))


def _hgrn_constants(L, n_sub):
    levels = []
    w = L // 2
    while w >= 1:
        levels.append(w)
        w //= 2
    t = np.arange(L)[:, None]
    u = np.arange(L)[None, :]
    blocks = [(u <= t)]
    for w in levels:
        start = (t // w) * w
        blocks.append((u > start) & (u <= t))
        nxt = np.minimum((t // w + 1) * w, L - 1)
        blocks.append((u > t) & (u <= nxt))
    g = np.concatenate(blocks, axis=0).astype(np.float32)
    g3 = np.concatenate([g, g, g], axis=1)
    level = np.full((L, L), -1, np.int32)
    s = u
    for i, w in enumerate(levels):
        hit = (t // (2 * w) == s // (2 * w)) & ((t // w) % 2 == 1) & ((s // w) % 2 == 0)
        level[hit] = i
    level[np.arange(L), np.arange(L)] = len(levels)
    wide = np.full((n_sub * L, n_sub * L), -1, np.int32)
    for j in range(n_sub):
        wide[j * L:(j + 1) * L, j * L:(j + 1) * L] = level
    return jnp.asarray(g3, BF16), jnp.asarray(wide, jnp.int32), len(levels)


def _diff_softmax_pv(s1, s2, lam, values):
    def probs(ss):
        m = functools.reduce(jnp.maximum, [jnp.max(s, axis=-1, keepdims=True) for s in ss])
        ps = [jnp.exp(s - m) for s in ss]
        l = functools.reduce(jnp.add, [jnp.sum(p, axis=-1, keepdims=True) for p in ps])
        return ps, l
    p1, l1 = probs(s1)
    p2, l2 = probs(s2)
    c1 = 1.0 / l1
    c2 = lam / l2
    return functools.reduce(jnp.add, [_mm((c1 * a - c2 * b).astype(BF16), v) for a, b, v in zip(p1, p2, values)])


def _mixer_body(cfg, *refs):
    T, P, L, n_sub, m_idx, lam_init, n_levels = cfg
    has_hist = P > 0
    R = n_sub * L
    (x_ref, w_ref, lbl_ref, ag_ref, bg_ref, dl_ref, bias_ref, g_ref, lev_ref), refs = refs[:9], refs[9:]
    if has_hist:
        (hk_ref, hv_ref, s0_ref), refs = refs[:3], refs[3:]
    (oa_ref, ob_ref, k_ref, v_ref, s_ref,
     xb_ref, proj_ref, st_ref, q1_ref, q2_ref, kb_ref, vb_ref) = refs
    h = pl.program_id(1)

    @pl.when(h == 0)
    def _():
        xb_ref[...] = x_ref[...].astype(BF16)

    rt = min(T, ROW_TILE)

    def proj_tile(i, carry):
        rows = pl.ds(pl.multiple_of(i * rt, rt), rt)
        proj_ref[rows, :] = _mm(xb_ref[rows, :], w_ref[...])
        return carry

    lax.fori_loop(0, T // rt, proj_tile, 0)

    lg = lbl_ref[...]
    e = jnp.exp(lg - jnp.max(lg, axis=0, keepdims=True))
    sm = e / jnp.sum(e, axis=0, keepdims=True)
    lb = jnp.sum(sm[0:m_idx + 1], axis=0, keepdims=True) - sm[0:1]
    a_gain = ag_ref[...]

    if has_hist:
        st_ref[...] = s0_ref[...].T
    else:
        st_ref[...] = jnp.zeros((HEAD_W, HEAD_W), F32)
    level = lev_ref[...]

    def wide(a):
        return a if n_sub == 1 else jnp.concatenate([a[j * L:(j + 1) * L] for j in range(n_sub)], axis=1)

    def hgrn_step(c, carry):
        rows = pl.ds(pl.multiple_of(c * R, R), R)
        q = proj_ref[rows, 0 * HEAD_W:1 * HEAD_W]
        zf = proj_ref[rows, 1 * HEAD_W:2 * HEAD_W]
        vi = proj_ref[rows, 2 * HEAD_W:3 * HEAD_W].astype(BF16)
        gate = proj_ref[rows, 3 * HEAD_W:4 * HEAD_W]
        log_f = jnp.log(lb + (1.0 - lb) * jax.nn.sigmoid(zf))
        k_in = (1.0 - lb) * jax.nn.sigmoid(-zf)
        hi = log_f.astype(BF16)
        r1 = log_f - hi.astype(F32)
        mid = r1.astype(BF16)
        lo = (r1 - mid.astype(F32)).astype(BF16)
        ex = _mm(g_ref[...], jnp.concatenate([wide(hi), wide(mid), wide(lo)], axis=0))

        def tall(k):
            blk = ex[k * L:(k + 1) * L]
            if n_sub == 1:
                return blk
            return jnp.concatenate([blk[:, j * HEAD_W:(j + 1) * HEAD_W] for j in range(n_sub)], axis=0)

        b = tall(0)
        scores = jnp.zeros((R, R), F32)
        for i in range(n_levels):
            qw = (q * jnp.exp(tall(1 + 2 * i))).astype(BF16)
            kw = (k_in * jnp.exp(tall(2 + 2 * i))).astype(BF16)
            scores = jnp.where(level == i, _nt(qw, kw), scores)
        scores = jnp.where(level == n_levels, _nt(q.astype(BF16), k_in.astype(BF16)), scores)
        o = _mm(scores.astype(BF16), vi)
        qd = (q * jnp.exp(b)).astype(BF16)
        st = st_ref[...]
        o_state = []
        for j in range(n_sub):
            sl = slice(j * L, (j + 1) * L)
            o_state.append(_nt(qd[sl], st.astype(BF16)))
            b_last = b[(j + 1) * L - 1:(j + 1) * L]
            kd = (k_in[sl] * jnp.exp(b_last - b[sl])).astype(BF16)
            st = st * jnp.exp(b_last) + _tn(vi[sl], kd)
        st_ref[...] = st
        o = o + (o_state[0] if n_sub == 1 else jnp.concatenate(o_state, axis=0))
        oa_ref[rows, :] = (_rms_norm(o, a_gain) * (gate * jax.nn.sigmoid(gate))).astype(BF16)
        return carry

    lax.fori_loop(0, T // R, hgrn_step, 0)
    s_ref[...] = st_ref[...].T

    lane = lax.broadcasted_iota(jnp.int32, (1, HEAD_W), 1)
    lp = dl_ref[...]
    lam = (jnp.exp(jnp.sum(lp[0:1] * lp[1:2], axis=-1, keepdims=True))
           - jnp.exp(jnp.sum(lp[2:3] * lp[3:4], axis=-1, keepdims=True)) + lam_init)
    qs = proj_ref[:, 4 * HEAD_W:5 * HEAD_W] * (B_DQK ** -0.5)
    q1_ref[...] = jnp.where(lane < B_DQK, qs, 0.0).astype(BF16)
    q2_ref[...] = jnp.where(lane >= B_DQK, qs, 0.0).astype(BF16)
    k_new = proj_ref[:, 5 * HEAD_W:6 * HEAD_W]
    v_new = proj_ref[:, 6 * HEAD_W:7 * HEAD_W]
    k_ref[...] = k_new
    v_ref[...] = v_new
    b_gain = bg_ref[...] * (1.0 - lam_init)

    if not has_hist:
        kb_ref[...] = k_new.astype(BF16)
        vb_ref[...] = v_new.astype(BF16)
        tq = min(T, ATT_TILE)
        for i in range(T // tq):
            segs = []
            if i >= 2:
                segs.append((0, (i - 1) * tq, None))
            if i >= 1:
                segs.append(((i - 1) * tq, i * tq, 1))
            segs.append((i * tq, (i + 1) * tq, 0))

            def scores_of(q):
                out = []
                for lo_k, hi_k, kind in segs:
                    s = _nt(q, kb_ref[lo_k:hi_k, :])
                    out.append(s if kind is None else s + bias_ref[kind])
                return out

            rows = slice(i * tq, (i + 1) * tq)
            o = _diff_softmax_pv(scores_of(q1_ref[rows, :]), scores_of(q2_ref[rows, :]), lam,
                                 [vb_ref[lo_k:hi_k, :] for lo_k, hi_k, _ in segs])
            ob_ref[rows, :] = (_rms_norm(o, 1.0) * b_gain).astype(BF16)
    else:
        kb_ref[...] = jnp.zeros(kb_ref.shape, BF16)
        vb_ref[...] = jnp.zeros(vb_ref.shape, BF16)
        kb_ref[0:T, :] = k_new.astype(BF16)
        vb_ref[0:T, :] = v_new.astype(BF16)
        tk = ATT_TILE
        k_far = hk_ref[0:P - tk, :].astype(BF16)
        k_near = hk_ref[P - tk:P, :].astype(BF16)
        bias_new = bias_ref[1, :, 0:HEAD_W]

        def scores_of(q):
            return [_nt(q, k_far), _nt(q, k_near) + bias_ref[0], _nt(q, kb_ref[...]) + bias_new]

        o = _diff_softmax_pv(scores_of(q1_ref[...]), scores_of(q2_ref[...]), lam,
                             [hv_ref[0:P - tk, :].astype(BF16), hv_ref[P - tk:P, :].astype(BF16), vb_ref[...]])
        ob_ref[...] = (_rms_norm(o, 1.0) * b_gain).astype(BF16)


def _mixer(x, w7, lb_logits, a_gain, b_gain, dlam, bias, hist, m_idx, lam_init):
    B, T, D = x.shape
    L = min(T, CHUNK)
    n_sub = min(T // L, HGRN_SUB)
    gmat, level, n_levels = _hgrn_constants(L, n_sub)
    P = 0 if hist is None else hist[0].shape[2]
    assert T % (n_sub * L) == 0 and T % min(T, ROW_TILE) == 0 and T % min(T, ATT_TILE) == 0
    assert ATT_TILE > MAX_DISTANCE and (T <= HEAD_W if hist is not None else True)
    assert P % ATT_TILE == 0 and (P == 0 or P >= 2 * ATT_TILE)
    cfg = (T, P, L, n_sub, m_idx, lam_init, n_levels)
    wcols = N_PROJ * HEAD_W
    in_specs = [
        pl.BlockSpec((None, T, D), lambda b, h: (b, 0, 0)),
        pl.BlockSpec((None, D, wcols), lambda b, h: (h, 0, 0)),
        pl.BlockSpec((None, N_MIX, HEAD_W), lambda b, h: (h, 0, 0)),
        pl.BlockSpec((None, 1, HEAD_W), lambda b, h: (h, 0, 0)),
        pl.BlockSpec((1, HEAD_W), lambda b, h: (0, 0)),
        pl.BlockSpec(dlam.shape, lambda b, h: (0, 0)),
        pl.BlockSpec((None,) + bias.shape[1:], lambda b, h: (h, 0, 0, 0)),
        pl.BlockSpec(gmat.shape, lambda b, h: (0, 0)),
        pl.BlockSpec(level.shape, lambda b, h: (0, 0)),
    ]
    args = [x, w7, lb_logits, a_gain, b_gain, dlam, bias, gmat, level]
    if hist is not None:
        hk, hv, s0 = hist
        in_specs += [
            pl.BlockSpec((None, None, P, HEAD_W), lambda b, h: (m_idx, b, 0, h)),
            pl.BlockSpec((None, None, P, HEAD_W), lambda b, h: (m_idx, b, 0, h)),
            pl.BlockSpec((None, None, None, HEAD_W, HEAD_W), lambda b, h: (m_idx, b, h, 0, 0)),
        ]
        args += [hk, hv, s0]
    col_spec = pl.BlockSpec((None, T, HEAD_W), lambda b, h: (b, 0, h))
    key_rows = T if hist is None else HEAD_W
    return pl.pallas_call(
        functools.partial(_mixer_body, cfg),
        grid=(B, HEADS),
        in_specs=in_specs,
        out_specs=[col_spec, col_spec, col_spec, col_spec,
                   pl.BlockSpec((None, None, HEAD_W, HEAD_W), lambda b, h: (b, h, 0, 0))],
        out_shape=[jax.ShapeDtypeStruct((B, T, A_W), BF16), jax.ShapeDtypeStruct((B, T, A_W), BF16),
                   jax.ShapeDtypeStruct((B, T, A_W), F32), jax.ShapeDtypeStruct((B, T, A_W), F32),
                   jax.ShapeDtypeStruct((B, HEADS, HEAD_W, HEAD_W), F32)],
        scratch_shapes=[pltpu.VMEM((T, D), BF16), pltpu.VMEM((T, wcols), F32),
                        pltpu.VMEM((HEAD_W, HEAD_W), F32),
                        pltpu.VMEM((T, HEAD_W), BF16), pltpu.VMEM((T, HEAD_W), BF16),
                        pltpu.VMEM((key_rows, HEAD_W), BF16), pltpu.VMEM((key_rows, HEAD_W), BF16)],
        compiler_params=_compiler_params(("arbitrary", "arbitrary")),
        name="mixer_ab",
    )(*args)


def _out_proj_body(x_ref, oa_ref, ob_ref, w_ref, g_ref, b_ref, out_ref):
    y = _mm(oa_ref[...], w_ref[0:A_W, :]) + _mm(ob_ref[...], w_ref[A_W:2 * A_W, :])
    out_ref[...] = _layer_norm(DN_ALPHA * x_ref[...] + y, g_ref[...], b_ref[...])


def _out_proj_ln(x, oa, ob, w_out, g, b):
    N, D = x.shape
    tm = min(N, ROW_TILE)
    row = lambda w: pl.BlockSpec((tm, w), lambda i: (i, 0))
    full = lambda a: pl.BlockSpec(a.shape, lambda i: (0,) * a.ndim)
    return pl.pallas_call(
        _out_proj_body,
        grid=(N // tm,),
        in_specs=[row(D), row(A_W), row(A_W), full(w_out), full(g), full(b)],
        out_specs=row(D),
        out_shape=jax.ShapeDtypeStruct((N, D), F32),
        compiler_params=_compiler_params(("arbitrary",)),
        name="out_proj_ln",
    )(x, oa, ob, w_out, g, b)


def _pool_body(cfg, x_ref, prev_ref, hist_ref, pw_ref, sc_ref, g_ref, b_ref, out_ref, ph_ref, xs_ref):
    tm, pos0 = cfg
    H = POOL_HIST + 1
    i = pl.program_id(1)
    x = x_ref[...]
    xs_ref[H:H + tm, :] = x

    @pl.when(i == 0)
    def _():
        xs_ref[0:1, :] = jnp.zeros((1, D_MODEL), F32)
        xs_ref[1:H, :] = hist_ref[...]

    @pl.when(i > 0)
    def _():
        xs_ref[0:H, :] = prev_ref[...]

    pos = pos0 + i * tm + lax.broadcasted_iota(jnp.int32, (tm, 1), 0)
    ys = []
    for g, w in enumerate(POOL_WINDOWS):
        cols = slice(g * POOL_GROUP, (g + 1) * POOL_GROUP)
        s = x[:, cols]
        for k in range(1, w):
            s = s + xs_ref[H - k:H - k + tm, cols]
        cnt = jnp.minimum(w, pos + 1).astype(F32)
        pooled = s / cnt - x[:, cols]
        ys.append(_mm(pooled.astype(BF16), pw_ref[g]))
    y = jnp.concatenate(ys, axis=-1) * sc_ref[...]
    out_ref[...] = _layer_norm(DN_ALPHA * x + y, g_ref[...], b_ref[...])
    ph_ref[...] = xs_ref[tm + 1:tm + H, :]


def _pool_ln(x, hist, pool_w, scale, g, b, pos0):
    B, T, D = x.shape
    H = POOL_HIST + 1
    tm = min(T, ROW_TILE)
    per = tm // H
    full = lambda a: pl.BlockSpec(a.shape, lambda bi, i: (0,) * a.ndim)
    return pl.pallas_call(
        functools.partial(_pool_body, (tm, pos0)),
        grid=(B, T // tm),
        in_specs=[pl.BlockSpec((None, tm, D), lambda bi, i: (bi, i, 0)),
                  pl.BlockSpec((None, H, D), lambda bi, i: (bi, jnp.maximum(i * per - 1, 0), 0)),
                  pl.BlockSpec((None, POOL_HIST, D), lambda bi, i: (bi, 0, 0)),
                  full(pool_w), full(scale), full(g), full(b)],
        out_specs=[pl.BlockSpec((None, tm, D), lambda bi, i: (bi, i, 0)),
                   pl.BlockSpec((None, POOL_HIST, D), lambda bi, i: (bi, 0, 0))],
        out_shape=[jax.ShapeDtypeStruct((B, T, D), F32), jax.ShapeDtypeStruct((B, POOL_HIST, D), F32)],
        scratch_shapes=[pltpu.VMEM((H + tm, D), F32)],
        compiler_params=_compiler_params(("arbitrary", "arbitrary")),
        name="pool_ln",
    )(x, x, hist, pool_w, scale, g, b)


def _ffn_body(cfg, x_ref, hist_ref, wup_ref, cw_ref, cb_ref, wdn_ref, g_ref, b_ref,
              out_ref, cs_ref, abuf_ref, carry_ref, gbuf_ref):
    tm, sh = cfg
    halo = (CONV_W - 1) * sh
    pad = -(-halo // 8) * 8
    n_chunks = D_FF // FF_CHUNK
    i = pl.program_id(1)
    x = x_ref[...]
    xb = x.astype(BF16)
    for c in range(n_chunks):
        ab = _mm(xb, wup_ref[c])
        a = ab[:, 0:FF_CHUNK]
        abuf_ref[pad:pad + tm, :] = a

        @pl.when(i == 0)
        def _():
            abuf_ref[pad - halo:pad, :] = hist_ref[c]

        @pl.when(i > 0)
        def _():
            abuf_ref[pad - halo:pad, :] = carry_ref[c]

        conv = cb_ref[c]
        for j in range(CONV_W - 1):
            lo = pad - halo + j * sh
            conv = conv + abuf_ref[lo:lo + tm, :] * cw_ref[c, j:j + 1, :]
        conv = conv + a * cw_ref[c, CONV_W - 1:CONV_W, :]
        tail = abuf_ref[pad + tm - halo:pad + tm, :]
        carry_ref[c] = tail
        cs_ref[c] = tail
        gbuf_ref[:, c * FF_CHUNK:(c + 1) * FF_CHUNK] = (
            conv * jax.nn.sigmoid(conv) * ab[:, FF_CHUNK:2 * FF_CHUNK]).astype(BF16)
    f = _mm(gbuf_ref[...], wdn_ref[...])
    out_ref[...] = _layer_norm(DN_ALPHA * x + f, g_ref[...], b_ref[...])


def _conv_ffn_ln(x, hist, w_up, conv_w, conv_b, w_down, g, b, sh):
    B, T, D = x.shape
    halo = (CONV_W - 1) * sh
    tm = min(T, ROW_TILE)
    n_chunks = D_FF // FF_CHUNK
    pad = -(-halo // 8) * 8
    hist_c = hist.reshape(B, halo, n_chunks, FF_CHUNK).transpose(0, 2, 1, 3)
    full = lambda a: pl.BlockSpec(a.shape, lambda bi, i: (0,) * a.ndim)
    out, cs = pl.pallas_call(
        functools.partial(_ffn_body, (tm, sh)),
        grid=(B, T // tm),
        in_specs=[pl.BlockSpec((None, tm, D), lambda bi, i: (bi, i, 0)),
                  pl.BlockSpec((None, n_chunks, halo, FF_CHUNK), lambda bi, i: (bi, 0, 0, 0)),
                  full(w_up), full(conv_w), full(conv_b), full(w_down), full(g), full(b)],
        out_specs=[pl.BlockSpec((None, tm, D), lambda bi, i: (bi, i, 0)),
                   pl.BlockSpec((None, n_chunks, halo, FF_CHUNK), lambda bi, i: (bi, 0, 0, 0))],
        out_shape=[jax.ShapeDtypeStruct((B, T, D), F32),
                   jax.ShapeDtypeStruct((B, n_chunks, halo, FF_CHUNK), F32)],
        scratch_shapes=[pltpu.VMEM((pad + tm, FF_CHUNK), F32),
                        pltpu.VMEM((n_chunks, halo, FF_CHUNK), F32),
                        pltpu.VMEM((tm, D_FF), BF16)],
        compiler_params=_compiler_params(("arbitrary", "arbitrary")),
        name="conv_ffn_ln",
    )(x, hist_c, w_up, conv_w, conv_b, w_down, g, b)
    return out, cs.transpose(0, 2, 1, 3).reshape(B, halo, D_FF)


def _prep_weights(w_in_ab, w_out_ab, lb_logits, hgrn_norm_g, diff_norm_g, pool_w, pool_scale, ffn_w_up,
                  ffn_conv_w, ffn_conv_b, ffn_w_down, ln1_g, ln1_b, ln2_g, ln2_b):
    n_chunks = D_FF // FF_CHUNK
    w = {}
    w["w7"] = (w_in_ab.astype(BF16).reshape(N_MIX, D_MODEL, N_PROJ, HEADS, HEAD_W)
               .transpose(0, 3, 1, 2, 4).reshape(N_MIX, HEADS, D_MODEL, N_PROJ * HEAD_W))
    w["w_out"] = w_out_ab.astype(BF16)
    w["lb_logits"] = lb_logits.astype(F32).reshape(N_MIX, HEADS, HEAD_W).transpose(1, 0, 2)
    w["a_gain"] = hgrn_norm_g.astype(F32).reshape(N_MIX, HEADS, 1, HEAD_W)
    w["b_gain"] = diff_norm_g.astype(F32).reshape(N_MIX, 1, HEAD_W)
    w["pool_w"] = pool_w.astype(BF16)
    w["pool_scale"] = pool_scale.astype(F32).reshape(-1, 1, D_MODEL)
    up = ffn_w_up.astype(BF16).reshape(DEPTH, D_MODEL, 2, n_chunks, FF_CHUNK)
    w["w_up"] = up.transpose(0, 3, 1, 2, 4).reshape(DEPTH, n_chunks, D_MODEL, 2 * FF_CHUNK)
    w["conv_w"] = ffn_conv_w.astype(F32).reshape(DEPTH, CONV_W, n_chunks, FF_CHUNK).transpose(0, 2, 1, 3)
    w["conv_b"] = ffn_conv_b.astype(F32).reshape(DEPTH, n_chunks, 1, FF_CHUNK)
    w["w_down"] = ffn_w_down.astype(BF16)
    for name, v in (("ln1_g", ln1_g), ("ln1_b", ln1_b), ("ln2_g", ln2_g), ("ln2_b", ln2_b)):
        w[name] = v.astype(F32).reshape(DEPTH, 1, D_MODEL)
    return w


def _attention_bias(rel_bias, T, P):
    if P == 0:
        tq = min(T, ATT_TILE)
        r = np.arange(tq)
        q_tiles = [tq + r] * 2
        k_tiles = [tq + r, r]
    else:
        r = P + np.arange(T)
        new = np.where(np.arange(ATT_TILE) < T, P + np.arange(ATT_TILE), -1)
        q_tiles = [r, r]
        k_tiles = [P - ATT_TILE + np.arange(ATT_TILE), new]
    return _bias_tiles(rel_bias, q_tiles, k_tiles)


def _trunk(x, k_hist, v_hist, s_hist, pool_hist, conv_hist, w, diff_lambda, rel_bias, time_major_ffn):
    B, T, D = x.shape
    P = 0 if k_hist is None else k_hist.shape[2]
    bias = _attention_bias(rel_bias, T, P)
    ks, vs, ss, ps, cs = [], [], [], [], []
    for l in range(DEPTH):
        if l % 2 == 0:
            m = l // 2
            lam_init = 0.8 - 0.6 * math.exp(-0.3 * l)
            hist = None
            if k_hist is not None:
                hist = (k_hist.reshape(N_MIX, B, P, A_W), v_hist.reshape(N_MIX, B, P, A_W), s_hist)
            oa, ob, k_new, v_new, s_new = _mixer(
                x, w["w7"][m], w["lb_logits"], w["a_gain"][m], w["b_gain"][m], diff_lambda[m].astype(F32),
                bias, hist, m, lam_init)
            ks.append(k_new.reshape(B, T, HEADS, HEAD_W))
            vs.append(v_new.reshape(B, T, HEADS, HEAD_W))
            ss.append(s_new)
            x = _out_proj_ln(x.reshape(B * T, D), oa.reshape(B * T, A_W), ob.reshape(B * T, A_W),
                             w["w_out"][m], w["ln1_g"][l], w["ln1_b"][l]).reshape(B, T, D)
        else:
            p = l // 2
            x, ph = _pool_ln(x, pool_hist[p], w["pool_w"][p], w["pool_scale"][p], w["ln1_g"][l], w["ln1_b"][l], P)
            ps.append(ph)
        ffn_w = (w["w_up"][l], w["conv_w"][l], w["conv_b"][l], w["w_down"][l], w["ln2_g"][l], w["ln2_b"][l])
        if time_major_ffn:
            xt = x.transpose(1, 0, 2).reshape(1, T * B, D)
            ht = conv_hist[l].transpose(1, 0, 2).reshape(1, (CONV_W - 1) * B, D_FF)
            xt, ch = _conv_ffn_ln(xt, ht, *ffn_w, sh=B)
            x = xt.reshape(T, B, D).transpose(1, 0, 2)
            ch = ch.reshape(CONV_W - 1, B, D_FF).transpose(1, 0, 2)
        else:
            x, ch = _conv_ffn_ln(x, conv_hist[l], *ffn_w, sh=1)
        cs.append(ch)
    return x, jnp.stack(ks), jnp.stack(vs), jnp.stack(ss), jnp.stack(ps), jnp.stack(cs)


def kernel(x_prompt, x_sample, cache_k, cache_v, state_hgrn, state_pool, state_ffn_conv, w_in_ab, w_out_ab,
           lb_logits, hgrn_norm_g, diff_lambda, diff_norm_g, rel_bias, pool_w, pool_scale, ffn_w_up,
           ffn_conv_w, ffn_conv_b, ffn_w_down, ln1_g, ln1_b, ln2_g, ln2_b):
    w = _prep_weights(w_in_ab, w_out_ab, lb_logits, hgrn_norm_g, diff_norm_g, pool_w, pool_scale, ffn_w_up,
                      ffn_conv_w, ffn_conv_b, ffn_w_down, ln1_g, ln1_b, ln2_g, ln2_b)
    Bp, Tp, _ = x_prompt.shape
    dt = x_prompt.dtype
    y_p, k_p, v_p, s_p, pool_p, conv_p = _trunk(
        x_prompt, None, None, None,
        jnp.zeros((DEPTH // 2, Bp, POOL_HIST, D_MODEL), dt),
        jnp.zeros((DEPTH, Bp, CONV_W - 1, D_FF), dt),
        w, diff_lambda, rel_bias, time_major_ffn=False)
    y_s, k_s, v_s, s_s, pool_s, conv_s = _trunk(
        x_sample, cache_k, cache_v, state_hgrn, state_pool, state_ffn_conv,
        w, diff_lambda, rel_bias, time_major_ffn=True)
    return (y_p, y_s, k_p, v_p, s_p, pool_p, conv_p, k_s, v_s, s_s, pool_s, conv_s)
```

```python
import functools
import math

import numpy as np
import jax
import jax.numpy as jnp
from jax import lax
from jax.experimental import pallas as pl
from jax.experimental.pallas import tpu as pltpu

F32 = jnp.float32
BF16 = jnp.bfloat16

D_MODEL = 1024
DEPTH = 4
CHUNK = 64
N_MIX = (DEPTH + 1) // 2
HEADS = 4
HEAD_W = 128
A_W = HEADS * HEAD_W
B_DQK = 64
POOL_WINDOWS = (2, 4, 8, 16)
POOL_GROUP = D_MODEL // len(POOL_WINDOWS)
POOL_HIST = max(POOL_WINDOWS) - 1
D_FF = 256 * math.ceil(8 * D_MODEL / 3 / 256)
CONV_W = 3
NUM_BUCKETS = 32
MAX_DISTANCE = 128
DN_ALPHA = (2 * DEPTH) ** 0.25
LN_EPS = 1e-5
RMS_EPS = 1e-6
MASK_NEG = -1e30

VMEM_LIMIT_BYTES = 56 * 1024 * 1024
ATT_TILE = 256
ROW_TILE = 512
FFN_ROW_TILE = 1024
FF_CHUNK = 256
N_PROJ = 7
HGRN_SUB = 4


def _nt(a, b):
    return lax.dot_general(a, b, (((1,), (1,)), ((), ())), preferred_element_type=F32)


def _tn(a, b):
    return lax.dot_general(a, b, (((0,), (0,)), ((), ())), preferred_element_type=F32)


def _mm(a, b):
    return jnp.dot(a, b, preferred_element_type=F32)


def _layer_norm(y, g, b):
    mu = jnp.mean(y, axis=-1, keepdims=True)
    yc = y - mu
    var = jnp.mean(yc * yc, axis=-1, keepdims=True)
    return yc * lax.rsqrt(var + LN_EPS) * g + b


def _rms_norm(y, g):
    return y * lax.rsqrt(jnp.mean(y * y, axis=-1, keepdims=True) + RMS_EPS) * g


def _compiler_params(semantics):
    return pltpu.CompilerParams(dimension_semantics=semantics, vmem_limit_bytes=VMEM_LIMIT_BYTES)


def _rel_bucket(rel):
    half = NUM_BUCKETS // 2
    max_exact = half // 2
    base = jnp.where(rel > 0, half, 0)
    n = jnp.abs(rel)
    nf = jnp.maximum(n, 1).astype(jnp.float32)
    large = max_exact + (jnp.log(nf / max_exact) / math.log(MAX_DISTANCE / max_exact)
                         * (half - max_exact)).astype(jnp.int32)
    large = jnp.minimum(large, half - 1)
    return base + jnp.where(n < max_exact, n, large)


def _bias_tile_body(far_bucket, bucket_ref, rb_ref, out_ref):
    h = pl.program_id(0)
    bucket = bucket_ref[...]
    acc = jnp.zeros(bucket.shape, F32)
    for b in range(NUM_BUCKETS):
        acc = jnp.where(bucket == b, rb_ref[b, h], acc)
    out_ref[...] = jnp.where(bucket < 0, MASK_NEG, acc - rb_ref[far_bucket, h])


def _bias_tiles(rel_bias, q_pos_tiles, k_pos_tiles):
    buckets = []
    for qp, kp in zip(q_pos_tiles, k_pos_tiles):
        qp = jnp.asarray(qp, jnp.int32)
        kp = jnp.asarray(kp, jnp.int32)
        visible = (kp[None, :] // CHUNK) <= (qp[:, None] // CHUNK)
        buckets.append(jnp.where(visible & (kp[None, :] >= 0), _rel_bucket(kp[None, :] - qp[:, None]), -1))
    bucket = jnp.stack(buckets).astype(jnp.int32)
    n, r, c = bucket.shape
    far_bucket = NUM_BUCKETS // 2 - 1
    return pl.pallas_call(
        functools.partial(_bias_tile_body, far_bucket),
        grid=(HEADS, n),
        in_specs=[pl.BlockSpec((None, r, c), lambda h, i: (i, 0, 0)),
                  pl.BlockSpec(memory_space=pltpu.SMEM)],
        out_specs=pl.BlockSpec((None, None, r, c), lambda h, i: (h, i, 0, 0)),
        out_shape=jax.ShapeDtypeStruct((HEADS, n, r, c), F32),
        name="bias_tiles",
    )(bucket, rel_bias.astype(F32))


def _hgrn_constants(L, n_sub):
    levels = []
    w = L // 2
    while w >= 1:
        levels.append(w)
        w //= 2
    t = np.arange(L)[:, None]
    u = np.arange(L)[None, :]
    blocks = [(u <= t)]
    for w in levels:
        start = (t // w) * w
        blocks.append((u > start) & (u <= t))
        nxt = np.minimum((t // w + 1) * w, L - 1)
        blocks.append((u > t) & (u <= nxt))
    g = np.concatenate(blocks, axis=0).astype(np.float32)
    g3 = np.concatenate([g, g, g], axis=1)
    level = np.full((L, L), -1, np.int32)
    s = u
    for i, w in enumerate(levels):
        hit = (t // (2 * w) == s // (2 * w)) & ((t // w) % 2 == 1) & ((s // w) % 2 == 0)
        level[hit] = i
    level[np.arange(L), np.arange(L)] = len(levels)
    wide = np.full((n_sub * L, n_sub * L), -1, np.int32)
    for j in range(n_sub):
        wide[j * L:(j + 1) * L, j * L:(j + 1) * L] = level
    return jnp.asarray(g3, BF16), jnp.asarray(wide, jnp.int32), len(levels)


def _diff_softmax_pv(s1, s2, lam, values):
    def probs(ss):
        m = functools.reduce(jnp.maximum, [jnp.max(s, axis=-1, keepdims=True) for s in ss])
        ps = [jnp.exp(s - m) for s in ss]
        l = functools.reduce(jnp.add, [jnp.sum(p, axis=-1, keepdims=True) for p in ps])
        return ps, l
    p1, l1 = probs(s1)
    p2, l2 = probs(s2)
    c1 = 1.0 / l1
    c2 = lam / l2
    return functools.reduce(jnp.add, [_mm((c1 * a - c2 * b).astype(BF16), v) for a, b, v in zip(p1, p2, values)])


def _mixer_body(cfg, *refs):
    T, P, L, n_sub, m_idx, lam_init, n_levels = cfg
    has_hist = P > 0
    R = n_sub * L
    (x_ref, w_ref, lbl_ref, ag_ref, bg_ref, dl_ref, bias_ref, g_ref, lev_ref), refs = refs[:9], refs[9:]
    if has_hist:
        (hk_ref, hv_ref, s0_ref), refs = refs[:3], refs[3:]
    if m_idx > 0:
        refs = refs[2:]
    (oa_ref, ob_ref, k_ref, v_ref, s_ref, proj_ref, st_ref, q1_ref, q2_ref, kb_ref, vb_ref), refs = refs[:11], refs[11:]
    h = pl.program_id(1)
    rt = min(T, ROW_TILE)

    def proj_tile(i, carry):
        rows = pl.ds(pl.multiple_of(i * rt, rt), rt)
        proj_ref[rows, :] = _mm(x_ref[rows, :], w_ref[...])
        return carry

    lax.fori_loop(0, T // rt, proj_tile, 0)

    lg = lbl_ref[...]
    e = jnp.exp(lg - jnp.max(lg, axis=0, keepdims=True))
    sm = e / jnp.sum(e, axis=0, keepdims=True)
    lb = jnp.sum(sm[0:m_idx + 1], axis=0, keepdims=True) - sm[0:1]
    a_gain = ag_ref[...]

    if has_hist:
        st_ref[...] = s0_ref[...].T
    else:
        st_ref[...] = jnp.zeros((HEAD_W, HEAD_W), F32)
    level = lev_ref[...]

    def wide(a):
        return a if n_sub == 1 else jnp.concatenate([a[j * L:(j + 1) * L] for j in range(n_sub)], axis=1)

    def hgrn_step(c, carry):
        rows = pl.ds(pl.multiple_of(c * R, R), R)
        q = proj_ref[rows, 0 * HEAD_W:1 * HEAD_W]
        zf = proj_ref[rows, 1 * HEAD_W:2 * HEAD_W]
        vi = proj_ref[rows, 2 * HEAD_W:3 * HEAD_W].astype(BF16)
        gate = proj_ref[rows, 3 * HEAD_W:4 * HEAD_W]
        log_f = jnp.log(lb + (1.0 - lb) * jax.nn.sigmoid(zf))
        k_in = (1.0 - lb) * jax.nn.sigmoid(-zf)
        hi = log_f.astype(BF16)
        r1 = log_f - hi.astype(F32)
        mid = r1.astype(BF16)
        lo = (r1 - mid.astype(F32)).astype(BF16)
        ex = _mm(g_ref[...], jnp.concatenate([wide(hi), wide(mid), wide(lo)], axis=0))

        def tall(k):
            blk = ex[k * L:(k + 1) * L]
            if n_sub == 1:
                return blk
            return jnp.concatenate([blk[:, j * HEAD_W:(j + 1) * HEAD_W] for j in range(n_sub)], axis=0)

        b = tall(0)
        scores = jnp.zeros((R, R), F32)
        for i in range(n_levels):
            qw = (q * jnp.exp(tall(1 + 2 * i))).astype(BF16)
            kw = (k_in * jnp.exp(tall(2 + 2 * i))).astype(BF16)
            scores = jnp.where(level == i, _nt(qw, kw), scores)
        scores = jnp.where(level == n_levels, _nt(q.astype(BF16), k_in.astype(BF16)), scores)
        o = _mm(scores.astype(BF16), vi)
        qd = (q * jnp.exp(b)).astype(BF16)
        st = st_ref[...]
        o_state = []
        for j in range(n_sub):
            sl = slice(j * L, (j + 1) * L)
            o_state.append(_nt(qd[sl], st.astype(BF16)))
            b_last = b[(j + 1) * L - 1:(j + 1) * L]
            kd = (k_in[sl] * jnp.exp(b_last - b[sl])).astype(BF16)
            st = st * jnp.exp(b_last) + _tn(vi[sl], kd)
        st_ref[...] = st
        o = o + (o_state[0] if n_sub == 1 else jnp.concatenate(o_state, axis=0))
        oa_ref[rows, :] = (_rms_norm(o, a_gain) * (gate * jax.nn.sigmoid(gate))).astype(BF16)
        return carry

    n_steps = T // R
    lax.fori_loop(0, n_steps, hgrn_step, 0, unroll=2 if n_steps % 2 == 0 else 1)
    s_ref[...] = st_ref[...].T

    lane = lax.broadcasted_iota(jnp.int32, (1, HEAD_W), 1)
    lp = dl_ref[...]
    lam = (jnp.exp(jnp.sum(lp[0:1] * lp[1:2], axis=-1, keepdims=True))
           - jnp.exp(jnp.sum(lp[2:3] * lp[3:4], axis=-1, keepdims=True)) + lam_init)
    qs = proj_ref[:, 4 * HEAD_W:5 * HEAD_W] * (B_DQK ** -0.5)
    q1_ref[...] = jnp.where(lane < B_DQK, qs, 0.0).astype(BF16)
    q2_ref[...] = jnp.where(lane >= B_DQK, qs, 0.0).astype(BF16)
    k_new = proj_ref[:, 5 * HEAD_W:6 * HEAD_W]
    v_new = proj_ref[:, 6 * HEAD_W:7 * HEAD_W]
    for hh in range(HEADS):
        @pl.when(h == hh)
        def _():
            k_ref[pl.ds(hh, T, stride=HEADS), :] = proj_ref[:, 5 * HEAD_W:6 * HEAD_W]
            v_ref[pl.ds(hh, T, stride=HEADS), :] = proj_ref[:, 6 * HEAD_W:7 * HEAD_W]
    b_gain = bg_ref[...] * (1.0 - lam_init)

    if not has_hist:
        kb_ref[...] = k_new.astype(BF16)
        vb_ref[...] = v_new.astype(BF16)
        tq = min(T, ATT_TILE)
        for i in range(T // tq):
            segs = []
            if i >= 2:
                segs.append((0, (i - 1) * tq, None))
            if i >= 1:
                segs.append(((i - 1) * tq, i * tq, 1))
            segs.append((i * tq, (i + 1) * tq, 0))

            def scores_of(q):
                out = []
                for lo_k, hi_k, kind in segs:
                    s = _nt(q, kb_ref[lo_k:hi_k, :])
                    out.append(s if kind is None else s + bias_ref[kind])
                return out

            rows = slice(i * tq, (i + 1) * tq)
            o = _diff_softmax_pv(scores_of(q1_ref[rows, :]), scores_of(q2_ref[rows, :]), lam,
                                 [vb_ref[lo_k:hi_k, :] for lo_k, hi_k, _ in segs])
            ob_ref[rows, :] = (_rms_norm(o, 1.0) * b_gain).astype(BF16)
    else:
        kb_ref[...] = jnp.zeros(kb_ref.shape, BF16)
        vb_ref[...] = jnp.zeros(vb_ref.shape, BF16)
        kb_ref[0:T, :] = k_new.astype(BF16)
        vb_ref[0:T, :] = v_new.astype(BF16)
        tk = ATT_TILE
        kd_ref, vd_ref = refs
        for hh in range(HEADS):
            @pl.when(h == hh)
            def _():
                kd_ref[...] = hk_ref[pl.ds(hh, P, stride=HEADS), :].astype(BF16)
                vd_ref[...] = hv_ref[pl.ds(hh, P, stride=HEADS), :].astype(BF16)
        bias_new = bias_ref[1, :, 0:HEAD_W]

        def scores_of(q):
            return [_nt(q, kd_ref[0:P - tk, :]), _nt(q, kd_ref[P - tk:P, :]) + bias_ref[0],
                    _nt(q, kb_ref[...]) + bias_new]

        o = _diff_softmax_pv(scores_of(q1_ref[...]), scores_of(q2_ref[...]), lam,
                             [vd_ref[0:P - tk, :], vd_ref[P - tk:P, :], vb_ref[...]])
        ob_ref[...] = (_rms_norm(o, 1.0) * b_gain).astype(BF16)


def _mixer(x, w7, lb_logits, a_gain, b_gain, dlam, bias, hist, kv_prev, m_idx, lam_init):
    B, T, D = x.shape
    L = min(T, CHUNK)
    n_sub = min(T // L, HGRN_SUB)
    gmat, level, n_levels = _hgrn_constants(L, n_sub)
    P = 0 if hist is None else hist[0].shape[2] // HEADS
    assert T % (n_sub * L) == 0 and T % min(T, ROW_TILE) == 0 and T % min(T, ATT_TILE) == 0
    assert ATT_TILE > MAX_DISTANCE and (T <= HEAD_W if hist is not None else True)
    assert P % ATT_TILE == 0 and (P == 0 or P >= 2 * ATT_TILE)
    cfg = (T, P, L, n_sub, m_idx, lam_init, n_levels)
    wcols = N_PROJ * HEAD_W
    in_specs = [
        pl.BlockSpec((None, T, D), lambda b, h: (b, 0, 0)),
        pl.BlockSpec((None, D, wcols), lambda b, h: (h, 0, 0)),
        pl.BlockSpec((None, N_MIX, HEAD_W), lambda b, h: (h, 0, 0)),
        pl.BlockSpec((None, 1, HEAD_W), lambda b, h: (h, 0, 0)),
        pl.BlockSpec((1, HEAD_W), lambda b, h: (0, 0)),
        pl.BlockSpec(dlam.shape, lambda b, h: (0, 0)),
        pl.BlockSpec((None,) + bias.shape[1:], lambda b, h: (h, 0, 0, 0)),
        pl.BlockSpec(gmat.shape, lambda b, h: (0, 0)),
        pl.BlockSpec(level.shape, lambda b, h: (0, 0)),
    ]
    args = [x, w7, lb_logits, a_gain, b_gain, dlam, bias, gmat, level]
    if hist is not None:
        hk, hv, s0 = hist
        in_specs += [
            pl.BlockSpec((None, None, P * HEADS, HEAD_W), lambda b, h: (m_idx, b, 0, 0)),
            pl.BlockSpec((None, None, P * HEADS, HEAD_W), lambda b, h: (m_idx, b, 0, 0)),
            pl.BlockSpec((None, None, None, HEAD_W, HEAD_W), lambda b, h: (m_idx, b, h, 0, 0)),
        ]
        args += [hk, hv, s0]
    aliases = {}
    if m_idx > 0:
        aliases = {len(args): 2, len(args) + 1: 3}
        in_specs += [pl.BlockSpec(memory_space=pl.ANY), pl.BlockSpec(memory_space=pl.ANY)]
        args += list(kv_prev)
    col_spec = pl.BlockSpec((None, T, HEAD_W), lambda b, h: (b, 0, h))
    kv_spec = pl.BlockSpec((None, None, T * HEADS, HEAD_W), lambda b, h: (m_idx, b, 0, 0))
    kv_shape = jax.ShapeDtypeStruct((N_MIX, B, T * HEADS, HEAD_W), F32)
    key_rows = T if hist is None else HEAD_W
    scratch = [pltpu.VMEM((T, wcols), F32), pltpu.VMEM((HEAD_W, HEAD_W), F32),
               pltpu.VMEM((T, HEAD_W), BF16), pltpu.VMEM((T, HEAD_W), BF16),
               pltpu.VMEM((key_rows, HEAD_W), BF16), pltpu.VMEM((key_rows, HEAD_W), BF16)]
    if hist is not None:
        scratch += [pltpu.VMEM((P, HEAD_W), BF16), pltpu.VMEM((P, HEAD_W), BF16)]
    return pl.pallas_call(
        functools.partial(_mixer_body, cfg),
        grid=(B, HEADS),
        in_specs=in_specs,
        out_specs=[col_spec, col_spec, kv_spec, kv_spec,
                   pl.BlockSpec((None, None, HEAD_W, HEAD_W), lambda b, h: (b, h, 0, 0))],
        out_shape=[jax.ShapeDtypeStruct((B, T, A_W), BF16), jax.ShapeDtypeStruct((B, T, A_W), BF16),
                   kv_shape, kv_shape, jax.ShapeDtypeStruct((B, HEADS, HEAD_W, HEAD_W), F32)],
        scratch_shapes=scratch,
        input_output_aliases=aliases,
        compiler_params=_compiler_params(("arbitrary", "arbitrary")),
        name="mixer_ab",
    )(*args)


def _out_proj_body(x_ref, oa_ref, ob_ref, w_ref, g_ref, b_ref, out_ref):
    y = _mm(oa_ref[...], w_ref[0:A_W, :]) + _mm(ob_ref[...], w_ref[A_W:2 * A_W, :])
    out_ref[...] = _layer_norm(DN_ALPHA * x_ref[...] + y, g_ref[...], b_ref[...])


def _out_proj_ln(x, oa, ob, w_out, g, b):
    N, D = x.shape
    tm = min(N, ROW_TILE)
    row = lambda w: pl.BlockSpec((tm, w), lambda i: (i, 0))
    full = lambda a: pl.BlockSpec(a.shape, lambda i: (0,) * a.ndim)
    return pl.pallas_call(
        _out_proj_body,
        grid=(N // tm,),
        in_specs=[row(D), row(A_W), row(A_W), full(w_out), full(g), full(b)],
        out_specs=row(D),
        out_shape=jax.ShapeDtypeStruct((N, D), F32),
        compiler_params=_compiler_params(("arbitrary",)),
        name="out_proj_ln",
    )(x, oa, ob, w_out, g, b)


def _pool_body(cfg, x_ref, prev_ref, hist_ref, pw_ref, sc_ref, g_ref, b_ref, out_ref, ph_ref, xs_ref):
    tm, pos0 = cfg
    H = POOL_HIST + 1
    i = pl.program_id(1)
    x = x_ref[...]
    xs_ref[H:H + tm, :] = x

    @pl.when(i == 0)
    def _():
        xs_ref[0:1, :] = jnp.zeros((1, D_MODEL), F32)
        xs_ref[1:H, :] = hist_ref[...]

    @pl.when(i > 0)
    def _():
        xs_ref[0:H, :] = prev_ref[...]

    pos = pos0 + i * tm + lax.broadcasted_iota(jnp.int32, (tm, 1), 0)
    ys = []
    for g, w in enumerate(POOL_WINDOWS):
        cols = slice(g * POOL_GROUP, (g + 1) * POOL_GROUP)
        s = x[:, cols]
        for k in range(1, w):
            s = s + xs_ref[H - k:H - k + tm, cols]
        cnt = jnp.minimum(w, pos + 1).astype(F32)
        pooled = s / cnt - x[:, cols]
        ys.append(_mm(pooled.astype(BF16), pw_ref[g]))
    y = jnp.concatenate(ys, axis=-1) * sc_ref[...]
    out_ref[...] = _layer_norm(DN_ALPHA * x + y, g_ref[...], b_ref[...])
    ph_ref[...] = xs_ref[tm + 1:tm + H, :]


def _pool_ln(x, hist, pool_w, scale, g, b, pos0):
    B, T, D = x.shape
    H = POOL_HIST + 1
    tm = min(T, ROW_TILE)
    per = tm // H
    full = lambda a: pl.BlockSpec(a.shape, lambda bi, i: (0,) * a.ndim)
    return pl.pallas_call(
        functools.partial(_pool_body, (tm, pos0)),
        grid=(B, T // tm),
        in_specs=[pl.BlockSpec((None, tm, D), lambda bi, i: (bi, i, 0)),
                  pl.BlockSpec((None, H, D), lambda bi, i: (bi, jnp.maximum(i * per - 1, 0), 0)),
                  pl.BlockSpec((None, POOL_HIST, D), lambda bi, i: (bi, 0, 0)),
                  full(pool_w), full(scale), full(g), full(b)],
        out_specs=[pl.BlockSpec((None, tm, D), lambda bi, i: (bi, i, 0)),
                   pl.BlockSpec((None, POOL_HIST, D), lambda bi, i: (bi, 0, 0))],
        out_shape=[jax.ShapeDtypeStruct((B, T, D), F32), jax.ShapeDtypeStruct((B, POOL_HIST, D), F32)],
        scratch_shapes=[pltpu.VMEM((H + tm, D), F32)],
        compiler_params=_compiler_params(("arbitrary", "arbitrary")),
        name="pool_ln",
    )(x, x, hist, pool_w, scale, g, b)


def _ffn_body(cfg, x_ref, hist_ref, wup_ref, cw_ref, cb_ref, wdn_ref, g_ref, b_ref, *refs):
    tm, sh, emit_bf16 = cfg
    if emit_bf16:
        out_ref, outb_ref, cs_ref, carry_ref, gbuf_ref = refs
    else:
        out_ref, cs_ref, carry_ref, gbuf_ref = refs
    halo = (CONV_W - 1) * sh
    pad = -(-halo // 8) * 8
    top = max(pad, 16)
    n_chunks = D_FF // FF_CHUNK
    i = pl.program_id(1)

    @pl.when(i == 0)
    def _():
        carry_ref[...] = hist_ref[...]

    x = x_ref[...]
    xb = x.astype(BF16)

    def conv_act(a_cur, shifted, gate, c):
        conv = cb_ref[c]
        for j in range(CONV_W - 1):
            conv = conv + shifted(j) * cw_ref[c, j:j + 1, :]
        conv = conv + a_cur * cw_ref[c, CONV_W - 1:CONV_W, :]
        return (conv * jax.nn.sigmoid(conv) * gate).astype(BF16)

    for c in range(n_chunks):
        cols = slice(c * FF_CHUNK, (c + 1) * FF_CHUNK)
        ab = _mm(xb, wup_ref[c])
        a = ab[:, 0:FF_CHUNK]
        gate = ab[:, FF_CHUNK:2 * FF_CHUNK]
        gbuf_ref[:, cols] = conv_act(a, lambda j: pltpu.roll(a, (CONV_W - 1 - j) * sh, 0), gate, c)
        ext = jnp.concatenate([carry_ref[c], a[0:top]], axis=0)
        gbuf_ref[0:top, cols] = conv_act(
            a[0:top], lambda j: pltpu.roll(ext, (CONV_W - 1 - j) * sh, 0)[pad:pad + top], gate[0:top], c)
        tail = a[tm - pad:tm]
        carry_ref[c] = tail
        cs_ref[c] = tail
    f = _mm(gbuf_ref[...], wdn_ref[...])
    y = _layer_norm(DN_ALPHA * x + f, g_ref[...], b_ref[...])
    out_ref[...] = y
    if emit_bf16:
        outb_ref[...] = y.astype(BF16)


def _conv_ffn_ln(x, hist, w_up, conv_w, conv_b, w_down, g, b, sh, emit_bf16):
    B, T, D = x.shape
    halo = (CONV_W - 1) * sh
    tm = min(T, FFN_ROW_TILE)
    n_chunks = D_FF // FF_CHUNK
    pad = -(-halo // 8) * 8
    assert T % tm == 0 and tm >= max(pad, 16)
    hist_c = hist.reshape(B, halo, n_chunks, FF_CHUNK).transpose(0, 2, 1, 3)
    hist_c = jnp.pad(hist_c, ((0, 0), (0, 0), (pad - halo, 0), (0, 0)))
    full = lambda a: pl.BlockSpec(a.shape, lambda bi, i: (0,) * a.ndim)
    row_spec = pl.BlockSpec((None, tm, D), lambda bi, i: (bi, i, 0))
    hist_spec = pl.BlockSpec((None, n_chunks, pad, FF_CHUNK), lambda bi, i: (bi, 0, 0, 0))
    out_specs = [row_spec] + ([row_spec] if emit_bf16 else []) + [hist_spec]
    out_shape = ([jax.ShapeDtypeStruct((B, T, D), F32)]
                 + ([jax.ShapeDtypeStruct((B, T, D), BF16)] if emit_bf16 else [])
                 + [jax.ShapeDtypeStruct((B, n_chunks, pad, FF_CHUNK), F32)])
    res = pl.pallas_call(
        functools.partial(_ffn_body, (tm, sh, emit_bf16)),
        grid=(B, T // tm),
        in_specs=[row_spec, hist_spec,
                  full(w_up), full(conv_w), full(conv_b), full(w_down), full(g), full(b)],
        out_specs=out_specs,
        out_shape=out_shape,
        scratch_shapes=[pltpu.VMEM((n_chunks, pad, FF_CHUNK), F32),
                        pltpu.VMEM((tm, D_FF), BF16)],
        compiler_params=_compiler_params(("arbitrary", "arbitrary")),
        name="conv_ffn_ln",
    )(x, hist_c, w_up, conv_w, conv_b, w_down, g, b)
    cs = res[-1][:, :, pad - halo:, :].transpose(0, 2, 1, 3).reshape(B, halo, D_FF)
    return res[:-1], cs


def _prep_weights(w_in_ab, w_out_ab, lb_logits, hgrn_norm_g, diff_norm_g, pool_w, pool_scale, ffn_w_up,
                  ffn_conv_w, ffn_conv_b, ffn_w_down, ln1_g, ln1_b, ln2_g, ln2_b):
    n_chunks = D_FF // FF_CHUNK
    w = {}
    w["w7"] = (w_in_ab.astype(BF16).reshape(N_MIX, D_MODEL, N_PROJ, HEADS, HEAD_W)
               .transpose(0, 3, 1, 2, 4).reshape(N_MIX, HEADS, D_MODEL, N_PROJ * HEAD_W))
    w["w_out"] = w_out_ab.astype(BF16)
    w["lb_logits"] = lb_logits.astype(F32).reshape(N_MIX, HEADS, HEAD_W).transpose(1, 0, 2)
    w["a_gain"] = hgrn_norm_g.astype(F32).reshape(N_MIX, HEADS, 1, HEAD_W)
    w["b_gain"] = diff_norm_g.astype(F32).reshape(N_MIX, 1, HEAD_W)
    w["pool_w"] = pool_w.astype(BF16)
    w["pool_scale"] = pool_scale.astype(F32).reshape(-1, 1, D_MODEL)
    up = ffn_w_up.astype(BF16).reshape(DEPTH, D_MODEL, 2, n_chunks, FF_CHUNK)
    w["w_up"] = up.transpose(0, 3, 1, 2, 4).reshape(DEPTH, n_chunks, D_MODEL, 2 * FF_CHUNK)
    w["conv_w"] = ffn_conv_w.astype(F32).reshape(DEPTH, CONV_W, n_chunks, FF_CHUNK).transpose(0, 2, 1, 3)
    w["conv_b"] = ffn_conv_b.astype(F32).reshape(DEPTH, n_chunks, 1, FF_CHUNK)
    w["w_down"] = ffn_w_down.astype(BF16)
    for name, v in (("ln1_g", ln1_g), ("ln1_b", ln1_b), ("ln2_g", ln2_g), ("ln2_b", ln2_b)):
        w[name] = v.astype(F32).reshape(DEPTH, 1, D_MODEL)
    return w


def _attention_bias(rel_bias, T, P):
    if P == 0:
        tq = min(T, ATT_TILE)
        r = np.arange(tq)
        q_tiles = [tq + r] * 2
        k_tiles = [tq + r, r]
    else:
        r = P + np.arange(T)
        new = np.where(np.arange(ATT_TILE) < T, P + np.arange(ATT_TILE), -1)
        q_tiles = [r, r]
        k_tiles = [P - ATT_TILE + np.arange(ATT_TILE), new]
    return _bias_tiles(rel_bias, q_tiles, k_tiles)


def _trunk(x, k_hist, v_hist, s_hist, pool_hist, conv_hist, w, diff_lambda, rel_bias, time_major_ffn):
    B, T, D = x.shape
    P = 0 if k_hist is None else k_hist.shape[2]
    bias = _attention_bias(rel_bias, T, P)
    hist = None
    if k_hist is not None:
        hist = (k_hist.reshape(N_MIX, B, P * HEADS, HEAD_W), v_hist.reshape(N_MIX, B, P * HEADS, HEAD_W), s_hist)
    kv = None
    xb = x.astype(BF16)
    ss, ps, cs = [], [], []
    for l in range(DEPTH):
        if l % 2 == 0:
            m = l // 2
            lam_init = 0.8 - 0.6 * math.exp(-0.3 * l)
            oa, ob, k_all, v_all, s_new = _mixer(
                xb, w["w7"][m], w["lb_logits"], w["a_gain"][m], w["b_gain"][m], diff_lambda[m].astype(F32),
                bias, hist, kv, m, lam_init)
            kv = (k_all, v_all)
            ss.append(s_new)
            x = _out_proj_ln(x.reshape(B * T, D), oa.reshape(B * T, A_W), ob.reshape(B * T, A_W),
                             w["w_out"][m], w["ln1_g"][l], w["ln1_b"][l]).reshape(B, T, D)
        else:
            p = l // 2
            x, ph = _pool_ln(x, pool_hist[p], w["pool_w"][p], w["pool_scale"][p], w["ln1_g"][l], w["ln1_b"][l], P)
            ps.append(ph)
        ffn_w = (w["w_up"][l], w["conv_w"][l], w["conv_b"][l], w["w_down"][l], w["ln2_g"][l], w["ln2_b"][l])
        want_bf16 = (l + 1 < DEPTH) and (l + 1) % 2 == 0
        if time_major_ffn:
            xt = x.transpose(1, 0, 2).reshape(1, T * B, D)
            ht = conv_hist[l].transpose(1, 0, 2).reshape(1, (CONV_W - 1) * B, D_FF)
            outs, ch = _conv_ffn_ln(xt, ht, *ffn_w, sh=B, emit_bf16=want_bf16)
            outs = [o.reshape(T, B, D).transpose(1, 0, 2) for o in outs]
            ch = ch.reshape(CONV_W - 1, B, D_FF).transpose(1, 0, 2)
        else:
            outs, ch = _conv_ffn_ln(x, conv_hist[l], *ffn_w, sh=1, emit_bf16=want_bf16)
        x = outs[0]
        if want_bf16:
            xb = outs[1]
        cs.append(ch)
    k_all = kv[0].reshape(N_MIX, B, T, HEADS, HEAD_W)
    v_all = kv[1].reshape(N_MIX, B, T, HEADS, HEAD_W)
    return x, k_all, v_all, jnp.stack(ss), jnp.stack(ps), jnp.stack(cs)


def kernel(x_prompt, x_sample, cache_k, cache_v, state_hgrn, state_pool, state_ffn_conv, w_in_ab, w_out_ab,
           lb_logits, hgrn_norm_g, diff_lambda, diff_norm_g, rel_bias, pool_w, pool_scale, ffn_w_up,
           ffn_conv_w, ffn_conv_b, ffn_w_down, ln1_g, ln1_b, ln2_g, ln2_b):
    w = _prep_weights(w_in_ab, w_out_ab, lb_logits, hgrn_norm_g, diff_norm_g, pool_w, pool_scale, ffn_w_up,
                      ffn_conv_w, ffn_conv_b, ffn_w_down, ln1_g, ln1_b, ln2_g, ln2_b)
    Bp, Tp, _ = x_prompt.shape
    dt = x_prompt.dtype
    y_p, k_p, v_p, s_p, pool_p, conv_p = _trunk(
        x_prompt, None, None, None,
        jnp.zeros((DEPTH // 2, Bp, POOL_HIST, D_MODEL), dt),
        jnp.zeros((DEPTH, Bp, CONV_W - 1, D_FF), dt),
        w, diff_lambda, rel_bias, time_major_ffn=False)
    y_s, k_s, v_s, s_s, pool_s, conv_s = _trunk(
        x_sample, cache_k, cache_v, state_hgrn, state_pool, state_ffn_conv,
        w, diff_lambda, rel_bias, time_major_ffn=True)
    return (y_p, y_s, k_p, v_p, s_p, pool_p, conv_p, k_s, v_s, s_s, pool_s, conv_s)
```

```python
import functools
import math

import numpy as np
import jax
import jax.numpy as jnp
from jax import lax
from jax.experimental import pallas as pl
from jax.experimental.pallas import tpu as pltpu

F32 = jnp.float32
BF16 = jnp.bfloat16

D_MODEL = 1024
DEPTH = 4
CHUNK = 64
N_MIX = (DEPTH + 1) // 2
HEADS = 4
HEAD_W = 128
A_W = HEADS * HEAD_W
B_DQK = 64
POOL_WINDOWS = (2, 4, 8, 16)
POOL_GROUP = D_MODEL // len(POOL_WINDOWS)
POOL_HIST = max(POOL_WINDOWS) - 1
D_FF = 256 * math.ceil(8 * D_MODEL / 3 / 256)
CONV_W = 3
NUM_BUCKETS = 32
MAX_DISTANCE = 128
DN_ALPHA = (2 * DEPTH) ** 0.25
LN_EPS = 1e-5
RMS_EPS = 1e-6
MASK_NEG = -1e30
LOG2E = 1.0 / math.log(2.0)

VMEM_LIMIT_BYTES = 56 * 1024 * 1024
ATT_TILE = 256
ROW_TILE = 512
FFN_ROW_TILE = 1024
FFN_SUB_TILE = 512
FF_CHUNK = 256
N_PROJ = 7
HGRN_SUB = 4


def _nt(a, b):
    return lax.dot_general(a, b, (((1,), (1,)), ((), ())), preferred_element_type=F32)


def _tn(a, b):
    return lax.dot_general(a, b, (((0,), (0,)), ((), ())), preferred_element_type=F32)


def _mm(a, b):
    return jnp.dot(a, b, preferred_element_type=F32)


def _layer_norm(y, g, b):
    mu = jnp.mean(y, axis=-1, keepdims=True)
    yc = y - mu
    var = jnp.mean(yc * yc, axis=-1, keepdims=True)
    return yc * lax.rsqrt(var + LN_EPS) * g + b


def _rms_norm(y, g):
    return y * lax.rsqrt(jnp.mean(y * y, axis=-1, keepdims=True) + RMS_EPS) * g


def _compiler_params(semantics):
    return pltpu.CompilerParams(dimension_semantics=semantics, vmem_limit_bytes=VMEM_LIMIT_BYTES)


def _rel_bucket(rel):
    half = NUM_BUCKETS // 2
    max_exact = half // 2
    base = jnp.where(rel > 0, half, 0)
    n = jnp.abs(rel)
    nf = jnp.maximum(n, 1).astype(jnp.float32)
    large = max_exact + (jnp.log(nf / max_exact) / math.log(MAX_DISTANCE / max_exact)
                         * (half - max_exact)).astype(jnp.int32)
    large = jnp.minimum(large, half - 1)
    return base + jnp.where(n < max_exact, n, large)


def _bias_tile_body(far_bucket, bucket_ref, rb_ref, out_ref):
    h = pl.program_id(0)
    bucket = bucket_ref[...]
    acc = jnp.zeros(bucket.shape, F32)
    for b in range(NUM_BUCKETS):
        acc = jnp.where(bucket == b, rb_ref[b, h], acc)
    out_ref[...] = jnp.where(bucket < 0, MASK_NEG, (acc - rb_ref[far_bucket, h]) * LOG2E)


def _bias_tiles(rel_bias, q_pos_tiles, k_pos_tiles):
    buckets = []
    for qp, kp in zip(q_pos_tiles, k_pos_tiles):
        qp = jnp.asarray(qp, jnp.int32)
        kp = jnp.asarray(kp, jnp.int32)
        visible = (kp[None, :] // CHUNK) <= (qp[:, None] // CHUNK)
        buckets.append(jnp.where(visible & (kp[None, :] >= 0), _rel_bucket(kp[None, :] - qp[:, None]), -1))
    bucket = jnp.stack(buckets).astype(jnp.int32)
    n, r, c = bucket.shape
    far_bucket = NUM_BUCKETS // 2 - 1
    return pl.pallas_call(
        functools.partial(_bias_tile_body, far_bucket),
        grid=(HEADS, n),
        in_specs=[pl.BlockSpec((None, r, c), lambda h, i: (i, 0, 0)),
                  pl.BlockSpec(memory_space=pltpu.SMEM)],
        out_specs=pl.BlockSpec((None, None, r, c), lambda h, i: (h, i, 0, 0)),
        out_shape=jax.ShapeDtypeStruct((HEADS, n, r, c), F32),
        name="bias_tiles",
    )(bucket, rel_bias.astype(F32))


def _hgrn_constants(L, n_sub):
    levels = []
    w = L // 2
    while w >= 1:
        levels.append(w)
        w //= 2
    t = np.arange(L)[:, None]
    u = np.arange(L)[None, :]
    blocks = [(u <= t)]
    for w in levels:
        start = (t // w) * w
        blocks.append((u > start) & (u <= t))
        nxt = np.minimum((t // w + 1) * w, L - 1)
        blocks.append((u > t) & (u <= nxt))
    g = np.concatenate(blocks, axis=0).astype(np.float32)
    g3 = np.concatenate([g, g, g], axis=1)
    level = np.full((L, L), -1, np.int32)
    s = u
    for i, w in enumerate(levels):
        hit = (t // (2 * w) == s // (2 * w)) & ((t // w) % 2 == 1) & ((s // w) % 2 == 0)
        level[hit] = i
    level[np.arange(L), np.arange(L)] = len(levels)
    wide = np.full((n_sub * L, n_sub * L), -1, np.int32)
    for j in range(n_sub):
        wide[j * L:(j + 1) * L, j * L:(j + 1) * L] = level
    return jnp.asarray(g3, BF16), jnp.asarray(wide, jnp.int32), len(levels)


def _diff_softmax_pv(s1, s2, lam, values):
    def probs(ss):
        m = functools.reduce(jnp.maximum, [jnp.max(s, axis=-1, keepdims=True) for s in ss])
        ps = [jnp.exp2(s - m) for s in ss]
        l = functools.reduce(jnp.add, [jnp.sum(p, axis=-1, keepdims=True) for p in ps])
        return ps, l
    p1, l1 = probs(s1)
    p2, l2 = probs(s2)
    c1 = 1.0 / l1
    c2 = lam / l2
    return functools.reduce(jnp.add, [_mm((c1 * a - c2 * b).astype(BF16), v) for a, b, v in zip(p1, p2, values)])


def _mixer_body(cfg, *refs):
    T, P, L, n_sub, m_idx, lam_init, n_levels = cfg
    has_hist = P > 0
    R = n_sub * L
    (x_ref, w_ref, lbl_ref, ag_ref, bg_ref, dl_ref, bias_ref, g_ref, lev_ref), refs = refs[:9], refs[9:]
    if has_hist:
        (hk_ref, hv_ref, s0_ref), refs = refs[:3], refs[3:]
    if m_idx > 0:
        refs = refs[2:]
    (oa_ref, ob_ref, k_ref, v_ref, s_ref, proj_ref, st_ref, q1_ref, q2_ref, kb_ref, vb_ref), refs = refs[:11], refs[11:]
    h = pl.program_id(1)
    n_tiles = T // R
    lane = lax.broadcasted_iota(jnp.int32, (1, HEAD_W), 1)

    def project(i):
        rows = slice(i * R, (i + 1) * R)
        proj = _mm(x_ref[rows, :], w_ref[...])
        proj_ref[rows, :] = proj
        qs = proj[:, 4 * HEAD_W:5 * HEAD_W] * (B_DQK ** -0.5 * LOG2E)
        q1_ref[rows, :] = jnp.where(lane < B_DQK, qs, 0.0).astype(BF16)
        q2_ref[rows, :] = jnp.where(lane >= B_DQK, qs, 0.0).astype(BF16)
        kb_ref[rows, :] = proj[:, 5 * HEAD_W:6 * HEAD_W].astype(BF16)
        vb_ref[rows, :] = proj[:, 6 * HEAD_W:7 * HEAD_W].astype(BF16)

    lg = lbl_ref[...]
    e = jnp.exp(lg - jnp.max(lg, axis=0, keepdims=True))
    sm = e / jnp.sum(e, axis=0, keepdims=True)
    lb = jnp.sum(sm[0:m_idx + 1], axis=0, keepdims=True) - sm[0:1]
    a_gain = ag_ref[...]

    if has_hist:
        st_ref[...] = s0_ref[...].T
    else:
        st_ref[...] = jnp.zeros((HEAD_W, HEAD_W), F32)
    level = lev_ref[...]

    def wide(a):
        return a if n_sub == 1 else jnp.concatenate([a[j * L:(j + 1) * L] for j in range(n_sub)], axis=1)

    def hgrn_step(c):
        rows = slice(c * R, (c + 1) * R)
        q = proj_ref[rows, 0 * HEAD_W:1 * HEAD_W]
        zf = proj_ref[rows, 1 * HEAD_W:2 * HEAD_W]
        vi = proj_ref[rows, 2 * HEAD_W:3 * HEAD_W].astype(BF16)
        gate = proj_ref[rows, 3 * HEAD_W:4 * HEAD_W]
        log_f = jnp.log(lb + (1.0 - lb) * jax.nn.sigmoid(zf))
        k_in = (1.0 - lb) * jax.nn.sigmoid(-zf)
        hi = log_f.astype(BF16)
        r1 = log_f - hi.astype(F32)
        mid = r1.astype(BF16)
        lo = (r1 - mid.astype(F32)).astype(BF16)
        ex = _mm(g_ref[...], jnp.concatenate([wide(hi), wide(mid), wide(lo)], axis=0))

        def tall(k):
            blk = ex[k * L:(k + 1) * L]
            if n_sub == 1:
                return blk
            return jnp.concatenate([blk[:, j * HEAD_W:(j + 1) * HEAD_W] for j in range(n_sub)], axis=0)

        b = tall(0)
        scores = jnp.zeros((R, R), F32)
        for i in range(n_levels):
            qw = (q * jnp.exp(tall(1 + 2 * i))).astype(BF16)
            kw = (k_in * jnp.exp(tall(2 + 2 * i))).astype(BF16)
            scores = jnp.where(level == i, _nt(qw, kw), scores)
        scores = jnp.where(level == n_levels, _nt(q.astype(BF16), k_in.astype(BF16)), scores)
        o = _mm(scores.astype(BF16), vi)
        qd = (q * jnp.exp(b)).astype(BF16)
        st = st_ref[...]
        o_state = []
        for j in range(n_sub):
            sl = slice(j * L, (j + 1) * L)
            o_state.append(_nt(qd[sl], st.astype(BF16)))
            b_last = b[(j + 1) * L - 1:(j + 1) * L]
            kd = (k_in[sl] * jnp.exp(b_last - b[sl])).astype(BF16)
            st = st * jnp.exp(b_last) + _tn(vi[sl], kd)
        st_ref[...] = st
        o = o + (o_state[0] if n_sub == 1 else jnp.concatenate(o_state, axis=0))
        oa_ref[rows, :] = (_rms_norm(o, a_gain) * (gate * jax.nn.sigmoid(gate))).astype(BF16)

    lp = dl_ref[...]
    lam = (jnp.exp(jnp.sum(lp[0:1] * lp[1:2], axis=-1, keepdims=True))
           - jnp.exp(jnp.sum(lp[2:3] * lp[3:4], axis=-1, keepdims=True)) + lam_init)
    b_gain = bg_ref[...] * (1.0 - lam_init)

    def attend(i):
        segs = []
        if i >= 2:
            segs.append((0, (i - 1) * R, None))
        if i >= 1:
            segs.append(((i - 1) * R, i * R, 1))
        segs.append((i * R, (i + 1) * R, 0))

        def scores_of(q):
            out = []
            for lo_k, hi_k, kind in segs:
                s = _nt(q, kb_ref[lo_k:hi_k, :])
                out.append(s if kind is None else s + bias_ref[kind])
            return out

        rows = slice(i * R, (i + 1) * R)
        o = _diff_softmax_pv(scores_of(q1_ref[rows, :]), scores_of(q2_ref[rows, :]), lam,
                             [vb_ref[lo_k:hi_k, :] for lo_k, hi_k, _ in segs])
        ob_ref[rows, :] = (_rms_norm(o, 1.0) * b_gain).astype(BF16)

    if has_hist:
        kb_ref[...] = jnp.zeros(kb_ref.shape, BF16)
        vb_ref[...] = jnp.zeros(vb_ref.shape, BF16)

    project(0)
    for i in range(n_tiles):
        if i + 1 < n_tiles:
            project(i + 1)
        hgrn_step(i)
        if not has_hist:
            attend(i)
    s_ref[...] = st_ref[...].T

    for hh in range(HEADS):
        @pl.when(h == hh)
        def _():
            k_ref[pl.ds(hh, T, stride=HEADS), :] = proj_ref[:, 5 * HEAD_W:6 * HEAD_W]
            v_ref[pl.ds(hh, T, stride=HEADS), :] = proj_ref[:, 6 * HEAD_W:7 * HEAD_W]

    if has_hist:
        tk = ATT_TILE
        kd_ref, vd_ref = refs
        for hh in range(HEADS):
            @pl.when(h == hh)
            def _():
                kd_ref[...] = hk_ref[pl.ds(hh, P, stride=HEADS), :].astype(BF16)
                vd_ref[...] = hv_ref[pl.ds(hh, P, stride=HEADS), :].astype(BF16)
        bias_new = bias_ref[1, :, 0:HEAD_W]

        def scores_of(q):
            return [_nt(q, kd_ref[0:P - tk, :]), _nt(q, kd_ref[P - tk:P, :]) + bias_ref[0],
                    _nt(q, kb_ref[...]) + bias_new]

        o = _diff_softmax_pv(scores_of(q1_ref[...]), scores_of(q2_ref[...]), lam,
                             [vd_ref[0:P - tk, :], vd_ref[P - tk:P, :], vb_ref[...]])
        ob_ref[...] = (_rms_norm(o, 1.0) * b_gain).astype(BF16)


def _mixer(x, w7, lb_logits, a_gain, b_gain, dlam, bias, hist, kv_prev, m_idx, lam_init):
    B, T, D = x.shape
    L = min(T, CHUNK)
    n_sub = min(T // L, HGRN_SUB)
    gmat, level, n_levels = _hgrn_constants(L, n_sub)
    P = 0 if hist is None else hist[0].shape[2] // HEADS
    assert T % (n_sub * L) == 0 and T % min(T, ROW_TILE) == 0 and T % min(T, ATT_TILE) == 0
    assert ATT_TILE > MAX_DISTANCE and (T <= HEAD_W if hist is not None else True)
    assert P % ATT_TILE == 0 and (P == 0 or P >= 2 * ATT_TILE)
    assert hist is not None or n_sub * L == min(T, ATT_TILE)
    cfg = (T, P, L, n_sub, m_idx, lam_init, n_levels)
    wcols = N_PROJ * HEAD_W
    in_specs = [
        pl.BlockSpec((None, T, D), lambda b, h: (b, 0, 0)),
        pl.BlockSpec((None, D, wcols), lambda b, h: (h, 0, 0)),
        pl.BlockSpec((None, N_MIX, HEAD_W), lambda b, h: (h, 0, 0)),
        pl.BlockSpec((None, 1, HEAD_W), lambda b, h: (h, 0, 0)),
        pl.BlockSpec((1, HEAD_W), lambda b, h: (0, 0)),
        pl.BlockSpec(dlam.shape, lambda b, h: (0, 0)),
        pl.BlockSpec((None,) + bias.shape[1:], lambda b, h: (h, 0, 0, 0)),
        pl.BlockSpec(gmat.shape, lambda b, h: (0, 0)),
        pl.BlockSpec(level.shape, lambda b, h: (0, 0)),
    ]
    args = [x, w7, lb_logits, a_gain, b_gain, dlam, bias, gmat, level]
    if hist is not None:
        hk, hv, s0 = hist
        in_specs += [
            pl.BlockSpec((None, None, P * HEADS, HEAD_W), lambda b, h: (m_idx, b, 0, 0)),
            pl.BlockSpec((None, None, P * HEADS, HEAD_W), lambda b, h: (m_idx, b, 0, 0)),
            pl.BlockSpec((None, None, None, HEAD_W, HEAD_W), lambda b, h: (m_idx, b, h, 0, 0)),
        ]
        args += [hk, hv, s0]
    aliases = {}
    if m_idx > 0:
        aliases = {len(args): 2, len(args) + 1: 3}
        in_specs += [pl.BlockSpec(memory_space=pl.ANY), pl.BlockSpec(memory_space=pl.ANY)]
        args += list(kv_prev)
    col_spec = pl.BlockSpec((None, T, HEAD_W), lambda b, h: (b, 0, h))
    kv_spec = pl.BlockSpec((None, None, T * HEADS, HEAD_W), lambda b, h: (m_idx, b, 0, 0))
    kv_shape = jax.ShapeDtypeStruct((N_MIX, B, T * HEADS, HEAD_W), F32)
    key_rows = T if hist is None else HEAD_W
    scratch = [pltpu.VMEM((T, wcols), F32), pltpu.VMEM((HEAD_W, HEAD_W), F32),
               pltpu.VMEM((T, HEAD_W), BF16), pltpu.VMEM((T, HEAD_W), BF16),
               pltpu.VMEM((key_rows, HEAD_W), BF16), pltpu.VMEM((key_rows, HEAD_W), BF16)]
    if hist is not None:
        scratch += [pltpu.VMEM((P, HEAD_W), BF16), pltpu.VMEM((P, HEAD_W), BF16)]
    return pl.pallas_call(
        functools.partial(_mixer_body, cfg),
        grid=(B, HEADS),
        in_specs=in_specs,
        out_specs=[col_spec, col_spec, kv_spec, kv_spec,
                   pl.BlockSpec((None, None, HEAD_W, HEAD_W), lambda b, h: (b, h, 0, 0))],
        out_shape=[jax.ShapeDtypeStruct((B, T, A_W), BF16), jax.ShapeDtypeStruct((B, T, A_W), BF16),
                   kv_shape, kv_shape, jax.ShapeDtypeStruct((B, HEADS, HEAD_W, HEAD_W), F32)],
        scratch_shapes=scratch,
        input_output_aliases=aliases,
        compiler_params=_compiler_params(("arbitrary", "arbitrary")),
        name="mixer_ab",
    )(*args)


def _out_proj_body(x_ref, oa_ref, ob_ref, w_ref, g_ref, b_ref, out_ref):
    y = _mm(oa_ref[...], w_ref[0:A_W, :]) + _mm(ob_ref[...], w_ref[A_W:2 * A_W, :])
    out_ref[...] = _layer_norm(DN_ALPHA * x_ref[...] + y, g_ref[...], b_ref[...])


def _out_proj_ln(x, oa, ob, w_out, g, b):
    N, D = x.shape
    tm = min(N, ROW_TILE)
    row = lambda w: pl.BlockSpec((tm, w), lambda i: (i, 0))
    full = lambda a: pl.BlockSpec(a.shape, lambda i: (0,) * a.ndim)
    return pl.pallas_call(
        _out_proj_body,
        grid=(N // tm,),
        in_specs=[row(D), row(A_W), row(A_W), full(w_out), full(g), full(b)],
        out_specs=row(D),
        out_shape=jax.ShapeDtypeStruct((N, D), F32),
        compiler_params=_compiler_params(("arbitrary",)),
        name="out_proj_ln",
    )(x, oa, ob, w_out, g, b)


def _pool_body(cfg, x_ref, prev_ref, hist_ref, pw_ref, sc_ref, g_ref, b_ref, out_ref, ph_ref, xs_ref):
    tm, pos0 = cfg
    H = POOL_HIST + 1
    i = pl.program_id(1)
    x = x_ref[...]
    xs_ref[H:H + tm, :] = x

    @pl.when(i == 0)
    def _():
        xs_ref[0:1, :] = jnp.zeros((1, D_MODEL), F32)
        xs_ref[1:H, :] = hist_ref[...]

    @pl.when(i > 0)
    def _():
        xs_ref[0:H, :] = prev_ref[...]

    pos = pos0 + i * tm + lax.broadcasted_iota(jnp.int32, (tm, 1), 0)
    ys = []
    for g, w in enumerate(POOL_WINDOWS):
        cols = slice(g * POOL_GROUP, (g + 1) * POOL_GROUP)
        s = x[:, cols]
        for k in range(1, w):
            s = s + xs_ref[H - k:H - k + tm, cols]
        cnt = jnp.minimum(w, pos + 1).astype(F32)
        pooled = s / cnt - x[:, cols]
        ys.append(_mm(pooled.astype(BF16), pw_ref[g]))
    y = jnp.concatenate(ys, axis=-1) * sc_ref[...]
    out_ref[...] = _layer_norm(DN_ALPHA * x + y, g_ref[...], b_ref[...])
    ph_ref[...] = xs_ref[tm + 1:tm + H, :]


def _pool_ln(x, hist, pool_w, scale, g, b, pos0):
    B, T, D = x.shape
    H = POOL_HIST + 1
    tm = min(T, ROW_TILE)
    per = tm // H
    full = lambda a: pl.BlockSpec(a.shape, lambda bi, i: (0,) * a.ndim)
    return pl.pallas_call(
        functools.partial(_pool_body, (tm, pos0)),
        grid=(B, T // tm),
        in_specs=[pl.BlockSpec((None, tm, D), lambda bi, i: (bi, i, 0)),
                  pl.BlockSpec((None, H, D), lambda bi, i: (bi, jnp.maximum(i * per - 1, 0), 0)),
                  pl.BlockSpec((None, POOL_HIST, D), lambda bi, i: (bi, 0, 0)),
                  full(pool_w), full(scale), full(g), full(b)],
        out_specs=[pl.BlockSpec((None, tm, D), lambda bi, i: (bi, i, 0)),
                   pl.BlockSpec((None, POOL_HIST, D), lambda bi, i: (bi, 0, 0))],
        out_shape=[jax.ShapeDtypeStruct((B, T, D), F32), jax.ShapeDtypeStruct((B, POOL_HIST, D), F32)],
        scratch_shapes=[pltpu.VMEM((H + tm, D), F32)],
        compiler_params=_compiler_params(("arbitrary", "arbitrary")),
        name="pool_ln",
    )(x, x, hist, pool_w, scale, g, b)


def _pooled_rows(x, halo, pos, pw_ref, scale):
    H = POOL_HIST + 1
    ys = []
    for g, w in enumerate(POOL_WINDOWS):
        cols = slice(g * POOL_GROUP, (g + 1) * POOL_GROUP)
        xg = x[:, cols]
        sg = xg
        eg = jnp.concatenate([halo[:, cols], xg[0:H]], axis=0)
        d = 1
        while d < w:
            sg = sg + pltpu.roll(sg, d, 0)
            eg = eg + pltpu.roll(eg, d, 0)
            d *= 2
        sg = jnp.concatenate([eg[H:2 * H], sg[H:]], axis=0)
        cnt = jnp.minimum(w, pos + 1).astype(F32)
        pooled = sg / cnt - xg
        ys.append(_mm(pooled.astype(BF16), pw_ref[g]))
    return jnp.concatenate(ys, axis=-1) * scale


def _ffn_body(cfg, *refs):
    tm, sh, emit_bf16, pre, pos0 = cfg
    x_ref, hist_ref, wup_ref, cw_ref, cb_ref, wdn_ref, g_ref, b_ref = refs[:8]
    refs = refs[8:]
    if pre == "proj":
        (oa_ref, ob_ref, wo_ref, g1_ref, b1_ref), refs = refs[:5], refs[5:]
    elif pre == "pool":
        (prev_ref, ph_in_ref, pw_ref, sc_ref, g1_ref, b1_ref), refs = refs[:6], refs[6:]
    out_ref, refs = refs[0], refs[1:]
    if emit_bf16:
        outb_ref, refs = refs[0], refs[1:]
    if pre == "pool":
        ph_ref, refs = refs[0], refs[1:]
    cs_ref, carry_ref, gbuf_ref = refs
    halo = (CONV_W - 1) * sh
    pad = -(-halo // 8) * 8
    top = max(pad, 16)
    n_chunks = D_FF // FF_CHUNK
    H = POOL_HIST + 1
    i = pl.program_id(1)

    @pl.when(i == 0)
    def _():
        carry_ref[...] = hist_ref[...]

    def conv_act(a_cur, shifted, gate, cols):
        conv = cb_ref[:, cols]
        for j in range(CONV_W - 1):
            conv = conv + shifted(j) * cw_ref[j:j + 1, cols]
        conv = conv + a_cur * cw_ref[CONV_W - 1:CONV_W, cols]
        return (conv * jax.nn.sigmoid(conv) * gate).astype(BF16)

    n_sub = tm // FFN_SUB_TILE if tm % FFN_SUB_TILE == 0 else 1
    ts = tm // n_sub
    for s in range(n_sub):
        rows = slice(s * ts, (s + 1) * ts)
        x = x_ref[rows, :]
        if pre == "proj":
            y = _mm(oa_ref[rows, :], wo_ref[0:A_W, :]) + _mm(ob_ref[rows, :], wo_ref[A_W:2 * A_W, :])
            x = _layer_norm(DN_ALPHA * x + y, g1_ref[...], b1_ref[...])
        elif pre == "pool":
            if s == 0:
                before = jnp.where(i == 0, ph_in_ref[...], prev_ref[...])
            else:
                before = x_ref[s * ts - H:s * ts, :]
            pos = pos0 + i * tm + s * ts + lax.broadcasted_iota(jnp.int32, (ts, 1), 0)
            y = _pooled_rows(x, before, pos, pw_ref, sc_ref[...])
            x = _layer_norm(DN_ALPHA * x + y, g1_ref[...], b1_ref[...])
        xb = x.astype(BF16)
        for c in range(n_chunks):
            cols = slice(c * FF_CHUNK, (c + 1) * FF_CHUNK)
            a = _mm(xb, wup_ref[:, cols])
            gate = _mm(xb, wup_ref[:, D_FF + c * FF_CHUNK:D_FF + (c + 1) * FF_CHUNK])
            gbuf_ref[rows, cols] = conv_act(a, lambda j: pltpu.roll(a, (CONV_W - 1 - j) * sh, 0), gate, cols)
            ext = jnp.concatenate([carry_ref[:, cols], a[0:top]], axis=0)
            gbuf_ref[s * ts:s * ts + top, cols] = conv_act(
                a[0:top], lambda j: pltpu.roll(ext, (CONV_W - 1 - j) * sh, 0)[pad:pad + top], gate[0:top], cols)
            carry_ref[:, cols] = a[ts - pad:ts]
        f = _mm(gbuf_ref[rows, :], wdn_ref[...])
        y = _layer_norm(DN_ALPHA * x + f, g_ref[...], b_ref[...])
        out_ref[rows, :] = y
        if emit_bf16:
            outb_ref[rows, :] = y.astype(BF16)
    cs_ref[...] = carry_ref[...]
    if pre == "pool":
        ph_ref[...] = x_ref[tm - POOL_HIST:tm, :]


def _conv_ffn_ln(x, hist, w_up, conv_w, conv_b, w_down, g, b, sh, emit_bf16, pre=None):
    B, T, D = x.shape
    halo = (CONV_W - 1) * sh
    tm = min(T, FFN_ROW_TILE)
    pad = -(-halo // 8) * 8
    H = POOL_HIST + 1
    assert T % tm == 0 and tm >= max(pad, 16)
    hist_p = jnp.pad(hist, ((0, 0), (pad - halo, 0), (0, 0)))
    full = lambda a: pl.BlockSpec(a.shape, lambda bi, i: (0,) * a.ndim)
    row = lambda w: pl.BlockSpec((None, tm, w), lambda bi, i: (bi, i, 0))
    hist_spec = pl.BlockSpec((None, pad, D_FF), lambda bi, i: (bi, 0, 0))
    kind, pos0, pre_args, pre_specs = None, 0, [], []
    if pre is not None:
        kind = pre[0]
        if kind == "proj":
            _, oa, ob, w_out, g1, b1 = pre
            pre_args = [oa, ob, w_out, g1, b1]
            pre_specs = [row(A_W), row(A_W), full(w_out), full(g1), full(b1)]
        else:
            _, pool_hist, pool_w, scale, g1, b1, pos0 = pre
            assert sh == 1 and tm % H == 0
            ph_in = jnp.pad(pool_hist, ((0, 0), (1, 0), (0, 0)))
            pre_args = [x, ph_in, pool_w, scale, g1, b1]
            pre_specs = [pl.BlockSpec((None, H, D), lambda bi, i: (bi, jnp.maximum(i * (tm // H) - 1, 0), 0)),
                         pl.BlockSpec((None, H, D), lambda bi, i: (bi, 0, 0)),
                         full(pool_w), full(scale), full(g1), full(b1)]
    out_specs = [row(D)] + ([row(D)] if emit_bf16 else [])
    out_shape = [jax.ShapeDtypeStruct((B, T, D), F32)] + ([jax.ShapeDtypeStruct((B, T, D), BF16)] if emit_bf16 else [])
    if kind == "pool":
        out_specs.append(pl.BlockSpec((None, POOL_HIST, D), lambda bi, i: (bi, 0, 0)))
        out_shape.append(jax.ShapeDtypeStruct((B, POOL_HIST, D), F32))
    out_specs.append(hist_spec)
    out_shape.append(jax.ShapeDtypeStruct((B, pad, D_FF), F32))
    res = pl.pallas_call(
        functools.partial(_ffn_body, (tm, sh, emit_bf16, kind, pos0)),
        grid=(B, T // tm),
        in_specs=[row(D), hist_spec,
                  full(w_up), full(conv_w), full(conv_b), full(w_down), full(g), full(b)] + pre_specs,
        out_specs=out_specs,
        out_shape=out_shape,
        scratch_shapes=[pltpu.VMEM((pad, D_FF), F32), pltpu.VMEM((tm, D_FF), BF16)],
        compiler_params=_compiler_params(("arbitrary", "arbitrary")),
        name="conv_ffn_ln",
    )(x, hist_p, w_up, conv_w, conv_b, w_down, g, b, *pre_args)
    return res[:-1], res[-1][:, pad - halo:, :]


def _prep_weights(w_in_ab, w_out_ab, lb_logits, hgrn_norm_g, diff_norm_g, pool_w, pool_scale, ffn_w_up,
                  ffn_conv_w, ffn_conv_b, ffn_w_down, ln1_g, ln1_b, ln2_g, ln2_b):
    w = {}
    w["w7"] = (w_in_ab.astype(BF16).reshape(N_MIX, D_MODEL, N_PROJ, HEADS, HEAD_W)
               .transpose(0, 3, 1, 2, 4).reshape(N_MIX, HEADS, D_MODEL, N_PROJ * HEAD_W))
    w["w_out"] = w_out_ab.astype(BF16)
    w["lb_logits"] = lb_logits.astype(F32).reshape(N_MIX, HEADS, HEAD_W).transpose(1, 0, 2)
    w["a_gain"] = hgrn_norm_g.astype(F32).reshape(N_MIX, HEADS, 1, HEAD_W)
    w["b_gain"] = diff_norm_g.astype(F32).reshape(N_MIX, 1, HEAD_W)
    w["pool_w"] = pool_w.astype(BF16)
    w["pool_scale"] = pool_scale.astype(F32).reshape(-1, 1, D_MODEL)
    w["w_up"] = ffn_w_up.astype(BF16)
    w["conv_w"] = ffn_conv_w.astype(F32)
    w["conv_b"] = ffn_conv_b.astype(F32).reshape(DEPTH, 1, D_FF)
    w["w_down"] = ffn_w_down.astype(BF16)
    for name, v in (("ln1_g", ln1_g), ("ln1_b", ln1_b), ("ln2_g", ln2_g), ("ln2_b", ln2_b)):
        w[name] = v.astype(F32).reshape(DEPTH, 1, D_MODEL)
    return w


def _attention_bias(rel_bias, T, P):
    if P == 0:
        tq = min(T, ATT_TILE)
        r = np.arange(tq)
        q_tiles = [tq + r] * 2
        k_tiles = [tq + r, r]
    else:
        r = P + np.arange(T)
        new = np.where(np.arange(ATT_TILE) < T, P + np.arange(ATT_TILE), -1)
        q_tiles = [r, r]
        k_tiles = [P - ATT_TILE + np.arange(ATT_TILE), new]
    return _bias_tiles(rel_bias, q_tiles, k_tiles)


def _trunk(x, k_hist, v_hist, s_hist, pool_hist, conv_hist, w, diff_lambda, rel_bias, time_major_ffn):
    B, T, D = x.shape
    P = 0 if k_hist is None else k_hist.shape[2]
    bias = _attention_bias(rel_bias, T, P)
    hist = None
    if k_hist is not None:
        hist = (k_hist.reshape(N_MIX, B, P * HEADS, HEAD_W), v_hist.reshape(N_MIX, B, P * HEADS, HEAD_W), s_hist)
    kv = None
    xb = x.astype(BF16)
    ss, ps, cs = [], [], []
    for l in range(DEPTH):
        if l % 2 == 0:
            m = l // 2
            lam_init = 0.8 - 0.6 * math.exp(-0.3 * l)
            oa, ob, k_all, v_all, s_new = _mixer(
                xb, w["w7"][m], w["lb_logits"], w["a_gain"][m], w["b_gain"][m], diff_lambda[m].astype(F32),
                bias, hist, kv, m, lam_init)
            kv = (k_all, v_all)
            ss.append(s_new)
            pre = ("proj", oa, ob, w["w_out"][m], w["ln1_g"][l], w["ln1_b"][l])
        else:
            p = l // 2
            pre = ("pool", pool_hist[p], w["pool_w"][p], w["pool_scale"][p], w["ln1_g"][l], w["ln1_b"][l], P)
        ffn_w = (w["w_up"][l], w["conv_w"][l], w["conv_b"][l], w["w_down"][l], w["ln2_g"][l], w["ln2_b"][l])
        want_bf16 = (l + 1 < DEPTH) and (l + 1) % 2 == 0
        if time_major_ffn:
            if pre[0] == "proj":
                x = _out_proj_ln(x.reshape(B * T, D), oa.reshape(B * T, A_W), ob.reshape(B * T, A_W),
                                 *pre[3:]).reshape(B, T, D)
            else:
                x, ph = _pool_ln(x, *pre[1:])
                ps.append(ph)
            xt = x.transpose(1, 0, 2).reshape(1, T * B, D)
            ht = conv_hist[l].transpose(1, 0, 2).reshape(1, (CONV_W - 1) * B, D_FF)
            outs, ch = _conv_ffn_ln(xt, ht, *ffn_w, sh=B, emit_bf16=want_bf16)
            outs = [o.reshape(T, B, D).transpose(1, 0, 2) for o in outs]
            ch = ch.reshape(CONV_W - 1, B, D_FF).transpose(1, 0, 2)
        else:
            outs, ch = _conv_ffn_ln(x, conv_hist[l], *ffn_w, sh=1, emit_bf16=want_bf16, pre=pre)
            if pre[0] == "pool":
                ps.append(outs[-1])
        x = outs[0]
        if want_bf16:
            xb = outs[1]
        cs.append(ch)
    k_all = kv[0].reshape(N_MIX, B, T, HEADS, HEAD_W)
    v_all = kv[1].reshape(N_MIX, B, T, HEADS, HEAD_W)
    return x, k_all, v_all, jnp.stack(ss), jnp.stack(ps), jnp.stack(cs)


def kernel(x_prompt, x_sample, cache_k, cache_v, state_hgrn, state_pool, state_ffn_conv, w_in_ab, w_out_ab,
           lb_logits, hgrn_norm_g, diff_lambda, diff_norm_g, rel_bias, pool_w, pool_scale, ffn_w_up,
           ffn_conv_w, ffn_conv_b, ffn_w_down, ln1_g, ln1_b, ln2_g, ln2_b):
    w = _prep_weights(w_in_ab, w_out_ab, lb_logits, hgrn_norm_g, diff_norm_g, pool_w, pool_scale, ffn_w_up,
                      ffn_conv_w, ffn_conv_b, ffn_w_down, ln1_g, ln1_b, ln2_g, ln2_b)
    Bp, Tp, _ = x_prompt.shape
    dt = x_prompt.dtype
    y_p, k_p, v_p, s_p, pool_p, conv_p = _trunk(
        x_prompt, None, None, None,
        jnp.zeros((DEPTH // 2, Bp, POOL_HIST, D_MODEL), dt),
        jnp.zeros((DEPTH, Bp, CONV_W - 1, D_FF), dt),
        w, diff_lambda, rel_bias, time_major_ffn=False)
    y_s, k_s, v_s, s_s, pool_s, conv_s = _trunk(
        x_sample, cache_k, cache_v, state_hgrn, state_pool, state_ffn_conv,
        w, diff_lambda, rel_bias, time_major_ffn=True)
    return (y_p, y_s, k_p, v_p, s_p, pool_p, conv_p, k_s, v_s, s_s, pool_s, conv_s)
```

```python
import functools
import math

import numpy as np
import jax
import jax.numpy as jnp
from jax import lax
from jax.experimental import pallas as pl
from jax.experimental.pallas import tpu as pltpu

F32 = jnp.float32
BF16 = jnp.bfloat16

D_MODEL = 1024
DEPTH = 4
CHUNK = 64
N_MIX = (DEPTH + 1) // 2
HEADS = 4
HEAD_W = 128
A_W = HEADS * HEAD_W
B_DQK = 64
POOL_WINDOWS = (2, 4, 8, 16)
POOL_GROUP = D_MODEL // len(POOL_WINDOWS)
POOL_HIST = max(POOL_WINDOWS) - 1
D_FF = 256 * math.ceil(8 * D_MODEL / 3 / 256)
CONV_W = 3
NUM_BUCKETS = 32
MAX_DISTANCE = 128
DN_ALPHA = (2 * DEPTH) ** 0.25
LN_EPS = 1e-5
RMS_EPS = 1e-6
MASK_NEG = -1e30
LOG2E = 1.0 / math.log(2.0)

VMEM_LIMIT_BYTES = 56 * 1024 * 1024
ATT_TILE = 256
ROW_TILE = 512
FFN_ROW_TILE = 1024
FFN_SUB_TILE = 512
FF_CHUNK = 256
N_PROJ = 7
HGRN_SUB = 4


def _nt(a, b):
    return lax.dot_general(a, b, (((1,), (1,)), ((), ())), preferred_element_type=F32)


def _tn(a, b):
    return lax.dot_general(a, b, (((0,), (0,)), ((), ())), preferred_element_type=F32)


def _mm(a, b):
    return jnp.dot(a, b, preferred_element_type=F32)


def _layer_norm(y, g, b):
    mu = jnp.mean(y, axis=-1, keepdims=True)
    yc = y - mu
    var = jnp.mean(yc * yc, axis=-1, keepdims=True)
    return yc * lax.rsqrt(var + LN_EPS) * g + b


def _rms_norm(y, g):
    return y * lax.rsqrt(jnp.mean(y * y, axis=-1, keepdims=True) + RMS_EPS) * g


def _compiler_params(semantics):
    return pltpu.CompilerParams(dimension_semantics=semantics, vmem_limit_bytes=VMEM_LIMIT_BYTES)


def _rel_bucket(rel):
    half = NUM_BUCKETS // 2
    max_exact = half // 2
    base = jnp.where(rel > 0, half, 0)
    n = jnp.abs(rel)
    nf = jnp.maximum(n, 1).astype(jnp.float32)
    large = max_exact + (jnp.log(nf / max_exact) / math.log(MAX_DISTANCE / max_exact)
                         * (half - max_exact)).astype(jnp.int32)
    large = jnp.minimum(large, half - 1)
    return base + jnp.where(n < max_exact, n, large)


def _bias_tile_body(far_bucket, bucket_ref, rb_ref, out_ref):
    h = pl.program_id(0)
    bucket = bucket_ref[...]
    acc = jnp.zeros(bucket.shape, F32)
    for b in range(NUM_BUCKETS):
        acc = jnp.where(bucket == b, rb_ref[b, h], acc)
    out_ref[...] = jnp.where(bucket < 0, MASK_NEG, (acc - rb_ref[far_bucket, h]) * LOG2E)


def _bias_tiles(rel_bias, q_pos_tiles, k_pos_tiles):
    buckets = []
    for qp, kp in zip(q_pos_tiles, k_pos_tiles):
        qp = jnp.asarray(qp, jnp.int32)
        kp = jnp.asarray(kp, jnp.int32)
        visible = (kp[None, :] // CHUNK) <= (qp[:, None] // CHUNK)
        buckets.append(jnp.where(visible & (kp[None, :] >= 0), _rel_bucket(kp[None, :] - qp[:, None]), -1))
    bucket = jnp.stack(buckets).astype(jnp.int32)
    n, r, c = bucket.shape
    far_bucket = NUM_BUCKETS // 2 - 1
    return pl.pallas_call(
        functools.partial(_bias_tile_body, far_bucket),
        grid=(HEADS, n),
        in_specs=[pl.BlockSpec((None, r, c), lambda h, i: (i, 0, 0)),
                  pl.BlockSpec(memory_space=pltpu.SMEM)],
        out_specs=pl.BlockSpec((None, None, r, c), lambda h, i: (h, i, 0, 0)),
        out_shape=jax.ShapeDtypeStruct((HEADS, n, r, c), F32),
        name="bias_tiles",
    )(bucket, rel_bias.astype(F32))


def _hgrn_constants(L, n_sub):
    levels = []
    w = L // 2
    while w >= 1:
        levels.append(w)
        w //= 2
    t = np.arange(L)[:, None]
    u = np.arange(L)[None, :]
    blocks = [(u <= t)]
    for w in levels:
        start = (t // w) * w
        blocks.append((u > start) & (u <= t))
        nxt = np.minimum((t // w + 1) * w, L - 1)
        blocks.append((u > t) & (u <= nxt))
    g = np.concatenate(blocks, axis=0).astype(np.float32)
    g3 = np.concatenate([g, g, g], axis=1)
    level = np.full((L, L), -1, np.int32)
    s = u
    for i, w in enumerate(levels):
        hit = (t // (2 * w) == s // (2 * w)) & ((t // w) % 2 == 1) & ((s // w) % 2 == 0)
        level[hit] = i
    level[np.arange(L), np.arange(L)] = len(levels)
    pair = np.concatenate([level] * min(n_sub, 2), axis=1)
    return jnp.asarray(g3, BF16), jnp.asarray(pair, jnp.int32), len(levels)


def _diff_softmax_pv(s1, s2, lam, values):
    def probs(ss):
        m = functools.reduce(jnp.maximum, [jnp.max(s, axis=-1, keepdims=True) for s in ss])
        ps = [jnp.exp2(s - m) for s in ss]
        l = functools.reduce(jnp.add, [jnp.sum(p, axis=-1, keepdims=True) for p in ps])
        return ps, l
    p1, l1 = probs(s1)
    p2, l2 = probs(s2)
    c1 = 1.0 / l1
    c2 = lam / l2
    return functools.reduce(jnp.add, [_mm((c1 * a - c2 * b).astype(BF16), v) for a, b, v in zip(p1, p2, values)])


def _mixer_body(cfg, *refs):
    T, P, L, n_sub, m_idx, lam_init, n_levels = cfg
    has_hist = P > 0
    R = n_sub * L
    (x_ref, w_ref, lbl_ref, ag_ref, bg_ref, dl_ref, bias_ref, g_ref, lev_ref), refs = refs[:9], refs[9:]
    if has_hist:
        (hk_ref, hv_ref, s0_ref), refs = refs[:3], refs[3:]
    if m_idx > 0:
        refs = refs[2:]
    (oa_ref, ob_ref, k_ref, v_ref, s_ref, proj_ref, st_ref, q1_ref, q2_ref, kb_ref, vb_ref), refs = refs[:11], refs[11:]
    h = pl.program_id(1)
    n_tiles = T // R
    lane = lax.broadcasted_iota(jnp.int32, (1, HEAD_W), 1)

    def run(jobs):
        jobs = list(jobs)
        while jobs:
            for job in list(jobs):
                try:
                    next(job)
                except StopIteration:
                    jobs.remove(job)

    def project(i):
        rows = slice(i * R, (i + 1) * R)
        for c0 in range(0, N_PROJ * HEAD_W, 2 * HEAD_W):
            c1 = min(c0 + 2 * HEAD_W, N_PROJ * HEAD_W)
            proj = _mm(x_ref[rows, :], w_ref[:, c0:c1])
            proj_ref[rows, c0:c1] = proj
            for k in range(c0 // HEAD_W, c1 // HEAD_W):
                blk = proj[:, k * HEAD_W - c0:(k + 1) * HEAD_W - c0]
                if k == 4:
                    qs = blk * (B_DQK ** -0.5 * LOG2E)
                    q1_ref[rows, :] = jnp.where(lane < B_DQK, qs, 0.0).astype(BF16)
                    q2_ref[rows, :] = jnp.where(lane >= B_DQK, qs, 0.0).astype(BF16)
                elif k == 5:
                    kb_ref[rows, :] = blk.astype(BF16)
                elif k == 6:
                    vb_ref[rows, :] = blk.astype(BF16)
            yield

    lg = lbl_ref[...]
    e = jnp.exp(lg - jnp.max(lg, axis=0, keepdims=True))
    sm = e / jnp.sum(e, axis=0, keepdims=True)
    lb = jnp.sum(sm[0:m_idx + 1], axis=0, keepdims=True) - sm[0:1]
    a_gain = ag_ref[...]

    if has_hist:
        st_ref[...] = s0_ref[...].T
    else:
        st_ref[...] = jnp.zeros((HEAD_W, HEAD_W), F32)
    level = lev_ref[...]

    def lanes(j, n=1):
        return slice(j * HEAD_W, (j + n) * HEAD_W)

    def side_by_side(parts):
        return parts[0] if len(parts) == 1 else jnp.concatenate(parts, axis=1)

    lb_w = side_by_side([lb] * n_sub)
    zero_blk = jnp.zeros((L, HEAD_W), BF16)
    groups = [list(range(j, min(j + 2, n_sub))) for j in range(0, n_sub, 2)]

    def block_diag(a, js):
        if len(js) == 1:
            return a[:, lanes(js[0])]
        return jnp.concatenate(
            [side_by_side([a[:, lanes(j)] if jj == j else zero_blk for jj in js]) for j in js], axis=0)

    def hgrn_step(c):
        def stream(k):
            return side_by_side([proj_ref[c * R + j * L:c * R + (j + 1) * L, lanes(k)] for j in range(n_sub)])

        q = stream(0)
        zf = stream(1)
        vi = stream(2).astype(BF16)
        gate = stream(3)
        log_f = jnp.log(lb_w + (1.0 - lb_w) * jax.nn.sigmoid(zf))
        k_in = (1.0 - lb_w) * jax.nn.sigmoid(-zf)
        hi = log_f.astype(BF16)
        r1 = log_f - hi.astype(F32)
        mid = r1.astype(BF16)
        lo = (r1 - mid.astype(F32)).astype(BF16)
        ex = _mm(g_ref[...], jnp.concatenate([hi, mid, lo], axis=0))
        b = ex[0:L]
        yield

        def pair_scores(qw, kw, js):
            return _nt(qw[:, lanes(js[0], len(js))], block_diag(kw, js))

        scores = [jnp.zeros((L, len(js) * L), F32) for js in groups]
        for i in range(n_levels + 1):
            if i < n_levels:
                qw = (q * jnp.exp(ex[(1 + 2 * i) * L:(2 + 2 * i) * L])).astype(BF16)
                kw = (k_in * jnp.exp(ex[(2 + 2 * i) * L:(3 + 2 * i) * L])).astype(BF16)
            else:
                qw, kw = q.astype(BF16), k_in.astype(BF16)
            for gi, js in enumerate(groups):
                lev = level if len(js) == level.shape[1] // L else level[:, 0:len(js) * L]
                scores[gi] = jnp.where(lev == i, pair_scores(qw, kw, js), scores[gi])
            if i % 2 == 1:
                yield
        o = _mm(side_by_side(scores).astype(BF16), block_diag(vi, list(range(n_sub))))
        qd = (q * jnp.exp(b)).astype(BF16)
        st = st_ref[...]
        o_state = []
        for j in range(n_sub):
            o_state.append(_nt(qd[:, lanes(j)], st.astype(BF16)))
            b_last = b[L - 1:L, lanes(j)]
            kd = (k_in[:, lanes(j)] * jnp.exp(b_last - b[:, lanes(j)])).astype(BF16)
            st = st * jnp.exp(b_last) + _tn(vi[:, lanes(j)], kd)
            if j % 2 == 1:
                yield
        st_ref[...] = st
        o = o + side_by_side(o_state)
        act = gate * jax.nn.sigmoid(gate)
        for j in range(n_sub):
            oa_ref[c * R + j * L:c * R + (j + 1) * L, :] = (
                _rms_norm(o[:, lanes(j)], a_gain) * act[:, lanes(j)]).astype(BF16)

    lp = dl_ref[...]
    lam = (jnp.exp(jnp.sum(lp[0:1] * lp[1:2], axis=-1, keepdims=True))
           - jnp.exp(jnp.sum(lp[2:3] * lp[3:4], axis=-1, keepdims=True)) + lam_init)
    b_gain = bg_ref[...] * (1.0 - lam_init)

    def attend(i):
        segs = []
        if i >= 2:
            segs.append((0, (i - 1) * R, None))
        if i >= 1:
            segs.append(((i - 1) * R, i * R, 1))
        segs.append((i * R, (i + 1) * R, 0))

        rows = slice(i * R, (i + 1) * R)
        stats = []
        for q_ref in (q1_ref, q2_ref):
            q = q_ref[rows, :]
            ss = []
            for lo_k, hi_k, kind in segs:
                s = _nt(q, kb_ref[lo_k:hi_k, :])
                ss.append(s if kind is None else s + bias_ref[kind])
            m = functools.reduce(jnp.maximum, [jnp.max(s, axis=-1, keepdims=True) for s in ss])
            yield
            ps = []
            for s in ss:
                ps.append(jnp.exp2(s - m))
                yield
            stats.append((ps, functools.reduce(jnp.add, [jnp.sum(p, axis=-1, keepdims=True) for p in ps])))
        (p1, l1), (p2, l2) = stats
        c1 = 1.0 / l1
        c2 = lam / l2
        o = None
        for a, b, (lo_k, hi_k, _) in zip(p1, p2, segs):
            part = _mm((c1 * a - c2 * b).astype(BF16), vb_ref[lo_k:hi_k, :])
            o = part if o is None else o + part
            yield
        ob_ref[rows, :] = (_rms_norm(o, 1.0) * b_gain).astype(BF16)

    if has_hist:
        kb_ref[...] = jnp.zeros(kb_ref.shape, BF16)
        vb_ref[...] = jnp.zeros(vb_ref.shape, BF16)

    run([project(0)])
    for i in range(n_tiles):
        jobs = [project(i + 1)] if i + 1 < n_tiles else []
        jobs.append(hgrn_step(i))
        if not has_hist:
            jobs.append(attend(i))
        run(jobs)
    s_ref[...] = st_ref[...].T

    for hh in range(HEADS):
        @pl.when(h == hh)
        def _():
            k_ref[pl.ds(hh, T, stride=HEADS), :] = proj_ref[:, 5 * HEAD_W:6 * HEAD_W]
            v_ref[pl.ds(hh, T, stride=HEADS), :] = proj_ref[:, 6 * HEAD_W:7 * HEAD_W]

    if has_hist:
        tk = ATT_TILE
        kd_ref, vd_ref = refs
        for hh in range(HEADS):
            @pl.when(h == hh)
            def _():
                kd_ref[...] = hk_ref[pl.ds(hh, P, stride=HEADS), :].astype(BF16)
                vd_ref[...] = hv_ref[pl.ds(hh, P, stride=HEADS), :].astype(BF16)
        bias_new = bias_ref[1, :, 0:HEAD_W]

        def scores_of(q):
            return [_nt(q, kd_ref[0:P - tk, :]), _nt(q, kd_ref[P - tk:P, :]) + bias_ref[0],
                    _nt(q, kb_ref[...]) + bias_new]

        o = _diff_softmax_pv(scores_of(q1_ref[...]), scores_of(q2_ref[...]), lam,
                             [vd_ref[0:P - tk, :], vd_ref[P - tk:P, :], vb_ref[...]])
        ob_ref[...] = (_rms_norm(o, 1.0) * b_gain).astype(BF16)


def _mixer(x, w7, lb_logits, a_gain, b_gain, dlam, bias, hist, kv_prev, m_idx, lam_init):
    B, T, D = x.shape
    L = min(T, CHUNK)
    n_sub = min(T // L, HGRN_SUB)
    gmat, level, n_levels = _hgrn_constants(L, n_sub)
    P = 0 if hist is None else hist[0].shape[2] // HEADS
    assert T % (n_sub * L) == 0 and T % min(T, ROW_TILE) == 0 and T % min(T, ATT_TILE) == 0
    assert ATT_TILE > MAX_DISTANCE and (T <= HEAD_W if hist is not None else True)
    assert P % ATT_TILE == 0 and (P == 0 or P >= 2 * ATT_TILE)
    assert hist is not None or n_sub * L == min(T, ATT_TILE)
    cfg = (T, P, L, n_sub, m_idx, lam_init, n_levels)
    wcols = N_PROJ * HEAD_W
    in_specs = [
        pl.BlockSpec((None, T, D), lambda b, h: (b, 0, 0)),
        pl.BlockSpec((None, D, wcols), lambda b, h: (h, 0, 0)),
        pl.BlockSpec((None, N_MIX, HEAD_W), lambda b, h: (h, 0, 0)),
        pl.BlockSpec((None, 1, HEAD_W), lambda b, h: (h, 0, 0)),
        pl.BlockSpec((1, HEAD_W), lambda b, h: (0, 0)),
        pl.BlockSpec(dlam.shape, lambda b, h: (0, 0)),
        pl.BlockSpec((None,) + bias.shape[1:], lambda b, h: (h, 0, 0, 0)),
        pl.BlockSpec(gmat.shape, lambda b, h: (0, 0)),
        pl.BlockSpec(level.shape, lambda b, h: (0, 0)),
    ]
    args = [x, w7, lb_logits, a_gain, b_gain, dlam, bias, gmat, level]
    if hist is not None:
        hk, hv, s0 = hist
        in_specs += [
            pl.BlockSpec((None, None, P * HEADS, HEAD_W), lambda b, h: (m_idx, b, 0, 0)),
            pl.BlockSpec((None, None, P * HEADS, HEAD_W), lambda b, h: (m_idx, b, 0, 0)),
            pl.BlockSpec((None, None, None, HEAD_W, HEAD_W), lambda b, h: (m_idx, b, h, 0, 0)),
        ]
        args += [hk, hv, s0]
    aliases = {}
    if m_idx > 0:
        aliases = {len(args): 2, len(args) + 1: 3}
        in_specs += [pl.BlockSpec(memory_space=pl.ANY), pl.BlockSpec(memory_space=pl.ANY)]
        args += list(kv_prev)
    col_spec = pl.BlockSpec((None, T, HEAD_W), lambda b, h: (b, 0, h))
    kv_spec = pl.BlockSpec((None, None, T * HEADS, HEAD_W), lambda b, h: (m_idx, b, 0, 0))
    kv_shape = jax.ShapeDtypeStruct((N_MIX, B, T * HEADS, HEAD_W), F32)
    key_rows = T if hist is None else HEAD_W
    scratch = [pltpu.VMEM((T, wcols), F32), pltpu.VMEM((HEAD_W, HEAD_W), F32),
               pltpu.VMEM((T, HEAD_W), BF16), pltpu.VMEM((T, HEAD_W), BF16),
               pltpu.VMEM((key_rows, HEAD_W), BF16), pltpu.VMEM((key_rows, HEAD_W), BF16)]
    if hist is not None:
        scratch += [pltpu.VMEM((P, HEAD_W), BF16), pltpu.VMEM((P, HEAD_W), BF16)]
    return pl.pallas_call(
        functools.partial(_mixer_body, cfg),
        grid=(B, HEADS),
        in_specs=in_specs,
        out_specs=[col_spec, col_spec, kv_spec, kv_spec,
                   pl.BlockSpec((None, None, HEAD_W, HEAD_W), lambda b, h: (b, h, 0, 0))],
        out_shape=[jax.ShapeDtypeStruct((B, T, A_W), BF16), jax.ShapeDtypeStruct((B, T, A_W), BF16),
                   kv_shape, kv_shape, jax.ShapeDtypeStruct((B, HEADS, HEAD_W, HEAD_W), F32)],
        scratch_shapes=scratch,
        input_output_aliases=aliases,
        compiler_params=_compiler_params(("arbitrary", "arbitrary")),
        name="mixer_ab",
    )(*args)


def _out_proj_body(x_ref, oa_ref, ob_ref, w_ref, g_ref, b_ref, out_ref):
    y = _mm(oa_ref[...], w_ref[0:A_W, :]) + _mm(ob_ref[...], w_ref[A_W:2 * A_W, :])
    out_ref[...] = _layer_norm(DN_ALPHA * x_ref[...] + y, g_ref[...], b_ref[...])


def _out_proj_ln(x, oa, ob, w_out, g, b):
    N, D = x.shape
    tm = min(N, ROW_TILE)
    row = lambda w: pl.BlockSpec((tm, w), lambda i: (i, 0))
    full = lambda a: pl.BlockSpec(a.shape, lambda i: (0,) * a.ndim)
    return pl.pallas_call(
        _out_proj_body,
        grid=(N // tm,),
        in_specs=[row(D), row(A_W), row(A_W), full(w_out), full(g), full(b)],
        out_specs=row(D),
        out_shape=jax.ShapeDtypeStruct((N, D), F32),
        compiler_params=_compiler_params(("arbitrary",)),
        name="out_proj_ln",
    )(x, oa, ob, w_out, g, b)


def _pool_body(cfg, x_ref, prev_ref, hist_ref, pw_ref, sc_ref, g_ref, b_ref, out_ref, ph_ref, xs_ref):
    tm, pos0 = cfg
    H = POOL_HIST + 1
    i = pl.program_id(1)
    x = x_ref[...]
    xs_ref[H:H + tm, :] = x

    @pl.when(i == 0)
    def _():
        xs_ref[0:1, :] = jnp.zeros((1, D_MODEL), F32)
        xs_ref[1:H, :] = hist_ref[...]

    @pl.when(i > 0)
    def _():
        xs_ref[0:H, :] = prev_ref[...]

    pos = pos0 + i * tm + lax.broadcasted_iota(jnp.int32, (tm, 1), 0)
    ys = []
    for g, w in enumerate(POOL_WINDOWS):
        cols = slice(g * POOL_GROUP, (g + 1) * POOL_GROUP)
        s = x[:, cols]
        for k in range(1, w):
            s = s + xs_ref[H - k:H - k + tm, cols]
        cnt = jnp.minimum(w, pos + 1).astype(F32)
        pooled = s / cnt - x[:, cols]
        ys.append(_mm(pooled.astype(BF16), pw_ref[g]))
    y = jnp.concatenate(ys, axis=-1) * sc_ref[...]
    out_ref[...] = _layer_norm(DN_ALPHA * x + y, g_ref[...], b_ref[...])
    ph_ref[...] = xs_ref[tm + 1:tm + H, :]


def _pool_ln(x, hist, pool_w, scale, g, b, pos0):
    B, T, D = x.shape
    H = POOL_HIST + 1
    tm = min(T, ROW_TILE)
    per = tm // H
    full = lambda a: pl.BlockSpec(a.shape, lambda bi, i: (0,) * a.ndim)
    return pl.pallas_call(
        functools.partial(_pool_body, (tm, pos0)),
        grid=(B, T // tm),
        in_specs=[pl.BlockSpec((None, tm, D), lambda bi, i: (bi, i, 0)),
                  pl.BlockSpec((None, H, D), lambda bi, i: (bi, jnp.maximum(i * per - 1, 0), 0)),
                  pl.BlockSpec((None, POOL_HIST, D), lambda bi, i: (bi, 0, 0)),
                  full(pool_w), full(scale), full(g), full(b)],
        out_specs=[pl.BlockSpec((None, tm, D), lambda bi, i: (bi, i, 0)),
                   pl.BlockSpec((None, POOL_HIST, D), lambda bi, i: (bi, 0, 0))],
        out_shape=[jax.ShapeDtypeStruct((B, T, D), F32), jax.ShapeDtypeStruct((B, POOL_HIST, D), F32)],
        scratch_shapes=[pltpu.VMEM((H + tm, D), F32)],
        compiler_params=_compiler_params(("arbitrary", "arbitrary")),
        name="pool_ln",
    )(x, x, hist, pool_w, scale, g, b)


def _pooled_rows(x, halo, pos, pw_ref, scale):
    H = POOL_HIST + 1
    ys = []
    for g, w in enumerate(POOL_WINDOWS):
        cols = slice(g * POOL_GROUP, (g + 1) * POOL_GROUP)
        xg = x[:, cols]
        sg = xg
        eg = jnp.concatenate([halo[:, cols], xg[0:H]], axis=0)
        d = 1
        while d < w:
            sg = sg + pltpu.roll(sg, d, 0)
            eg = eg + pltpu.roll(eg, d, 0)
            d *= 2
        sg = jnp.concatenate([eg[H:2 * H], sg[H:]], axis=0)
        cnt = jnp.minimum(w, pos + 1).astype(F32)
        pooled = sg / cnt - xg
        ys.append(_mm(pooled.astype(BF16), pw_ref[g]))
    return jnp.concatenate(ys, axis=-1) * scale


def _ffn_body(cfg, *refs):
    tm, sh, emit_bf16, pre, pos0 = cfg
    x_ref, hist_ref, wup_ref, cw_ref, cb_ref, wdn_ref, g_ref, b_ref = refs[:8]
    refs = refs[8:]
    if pre == "proj":
        (oa_ref, ob_ref, wo_ref, g1_ref, b1_ref), refs = refs[:5], refs[5:]
    elif pre == "pool":
        (prev_ref, ph_in_ref, pw_ref, sc_ref, g1_ref, b1_ref), refs = refs[:6], refs[6:]
    out_ref, refs = refs[0], refs[1:]
    if emit_bf16:
        outb_ref, refs = refs[0], refs[1:]
    if pre == "pool":
        ph_ref, refs = refs[0], refs[1:]
    cs_ref, carry_ref, gbuf_ref = refs
    halo = (CONV_W - 1) * sh
    pad = -(-halo // 8) * 8
    top = max(pad, 16)
    n_chunks = D_FF // FF_CHUNK
    H = POOL_HIST + 1
    i = pl.program_id(1)

    @pl.when(i == 0)
    def _():
        carry_ref[...] = hist_ref[...]

    def conv_act(a_cur, shifted, gate, cols):
        conv = cb_ref[:, cols]
        for j in range(CONV_W - 1):
            conv = conv + shifted(j) * cw_ref[j:j + 1, cols]
        conv = conv + a_cur * cw_ref[CONV_W - 1:CONV_W, cols]
        return (conv * jax.nn.sigmoid(conv) * gate).astype(BF16)

    n_sub = tm // FFN_SUB_TILE if tm % FFN_SUB_TILE == 0 else 1
    ts = tm // n_sub
    for s in range(n_sub):
        rows = slice(s * ts, (s + 1) * ts)
        x = x_ref[rows, :]
        if pre == "proj":
            y = _mm(oa_ref[rows, :], wo_ref[0:A_W, :]) + _mm(ob_ref[rows, :], wo_ref[A_W:2 * A_W, :])
            x = _layer_norm(DN_ALPHA * x + y, g1_ref[...], b1_ref[...])
        elif pre == "pool":
            if s == 0:
                before = jnp.where(i == 0, ph_in_ref[...], prev_ref[...])
            else:
                before = x_ref[s * ts - H:s * ts, :]
            pos = pos0 + i * tm + s * ts + lax.broadcasted_iota(jnp.int32, (ts, 1), 0)
            y = _pooled_rows(x, before, pos, pw_ref, sc_ref[...])
            x = _layer_norm(DN_ALPHA * x + y, g1_ref[...], b1_ref[...])
        xb = x.astype(BF16)
        for c in range(n_chunks):
            cols = slice(c * FF_CHUNK, (c + 1) * FF_CHUNK)
            a = _mm(xb, wup_ref[:, cols])
            gate = _mm(xb, wup_ref[:, D_FF + c * FF_CHUNK:D_FF + (c + 1) * FF_CHUNK])
            gbuf_ref[rows, cols] = conv_act(a, lambda j: pltpu.roll(a, (CONV_W - 1 - j) * sh, 0), gate, cols)
            ext = jnp.concatenate([carry_ref[:, cols], a[0:top]], axis=0)
            gbuf_ref[s * ts:s * ts + top, cols] = conv_act(
                a[0:top], lambda j: pltpu.roll(ext, (CONV_W - 1 - j) * sh, 0)[pad:pad + top], gate[0:top], cols)
            carry_ref[:, cols] = a[ts - pad:ts]
        f = _mm(gbuf_ref[rows, :], wdn_ref[...])
        y = _layer_norm(DN_ALPHA * x + f, g_ref[...], b_ref[...])
        out_ref[rows, :] = y
        if emit_bf16:
            outb_ref[rows, :] = y.astype(BF16)
    cs_ref[...] = carry_ref[...]
    if pre == "pool":
        ph_ref[...] = x_ref[tm - POOL_HIST:tm, :]


def _conv_ffn_ln(x, hist, w_up, conv_w, conv_b, w_down, g, b, sh, emit_bf16, pre=None):
    B, T, D = x.shape
    halo = (CONV_W - 1) * sh
    tm = min(T, FFN_ROW_TILE)
    pad = -(-halo // 8) * 8
    H = POOL_HIST + 1
    assert T % tm == 0 and tm >= max(pad, 16)
    hist_p = jnp.pad(hist, ((0, 0), (pad - halo, 0), (0, 0)))
    full = lambda a: pl.BlockSpec(a.shape, lambda bi, i: (0,) * a.ndim)
    row = lambda w: pl.BlockSpec((None, tm, w), lambda bi, i: (bi, i, 0))
    hist_spec = pl.BlockSpec((None, pad, D_FF), lambda bi, i: (bi, 0, 0))
    kind, pos0, pre_args, pre_specs = None, 0, [], []
    if pre is not None:
        kind = pre[0]
        if kind == "proj":
            _, oa, ob, w_out, g1, b1 = pre
            pre_args = [oa, ob, w_out, g1, b1]
            pre_specs = [row(A_W), row(A_W), full(w_out), full(g1), full(b1)]
        else:
            _, pool_hist, pool_w, scale, g1, b1, pos0 = pre
            assert sh == 1 and tm % H == 0
            ph_in = jnp.pad(pool_hist, ((0, 0), (1, 0), (0, 0)))
            pre_args = [x, ph_in, pool_w, scale, g1, b1]
            pre_specs = [pl.BlockSpec((None, H, D), lambda bi, i: (bi, jnp.maximum(i * (tm // H) - 1, 0), 0)),
                         pl.BlockSpec((None, H, D), lambda bi, i: (bi, 0, 0)),
                         full(pool_w), full(scale), full(g1), full(b1)]
    out_specs = [row(D)] + ([row(D)] if emit_bf16 else [])
    out_shape = [jax.ShapeDtypeStruct((B, T, D), F32)] + ([jax.ShapeDtypeStruct((B, T, D), BF16)] if emit_bf16 else [])
    if kind == "pool":
        out_specs.append(pl.BlockSpec((None, POOL_HIST, D), lambda bi, i: (bi, 0, 0)))
        out_shape.append(jax.ShapeDtypeStruct((B, POOL_HIST, D), F32))
    out_specs.append(hist_spec)
    out_shape.append(jax.ShapeDtypeStruct((B, pad, D_FF), F32))
    res = pl.pallas_call(
        functools.partial(_ffn_body, (tm, sh, emit_bf16, kind, pos0)),
        grid=(B, T // tm),
        in_specs=[row(D), hist_spec,
                  full(w_up), full(conv_w), full(conv_b), full(w_down), full(g), full(b)] + pre_specs,
        out_specs=out_specs,
        out_shape=out_shape,
        scratch_shapes=[pltpu.VMEM((pad, D_FF), F32), pltpu.VMEM((tm, D_FF), BF16)],
        compiler_params=_compiler_params(("arbitrary", "arbitrary")),
        name="conv_ffn_ln",
    )(x, hist_p, w_up, conv_w, conv_b, w_down, g, b, *pre_args)
    return res[:-1], res[-1][:, pad - halo:, :]


def _prep_weights(w_in_ab, w_out_ab, lb_logits, hgrn_norm_g, diff_norm_g, pool_w, pool_scale, ffn_w_up,
                  ffn_conv_w, ffn_conv_b, ffn_w_down, ln1_g, ln1_b, ln2_g, ln2_b):
    w = {}
    w["w7"] = (w_in_ab.astype(BF16).reshape(N_MIX, D_MODEL, N_PROJ, HEADS, HEAD_W)
               .transpose(0, 3, 1, 2, 4).reshape(N_MIX, HEADS, D_MODEL, N_PROJ * HEAD_W))
    w["w_out"] = w_out_ab.astype(BF16)
    w["lb_logits"] = lb_logits.astype(F32).reshape(N_MIX, HEADS, HEAD_W).transpose(1, 0, 2)
    w["a_gain"] = hgrn_norm_g.astype(F32).reshape(N_MIX, HEADS, 1, HEAD_W)
    w["b_gain"] = diff_norm_g.astype(F32).reshape(N_MIX, 1, HEAD_W)
    w["pool_w"] = pool_w.astype(BF16)
    w["pool_scale"] = pool_scale.astype(F32).reshape(-1, 1, D_MODEL)
    w["w_up"] = ffn_w_up.astype(BF16)
    w["conv_w"] = ffn_conv_w.astype(F32)
    w["conv_b"] = ffn_conv_b.astype(F32).reshape(DEPTH, 1, D_FF)
    w["w_down"] = ffn_w_down.astype(BF16)
    for name, v in (("ln1_g", ln1_g), ("ln1_b", ln1_b), ("ln2_g", ln2_g), ("ln2_b", ln2_b)):
        w[name] = v.astype(F32).reshape(DEPTH, 1, D_MODEL)
    return w


def _attention_bias(rel_bias, T, P):
    if P == 0:
        tq = min(T, ATT_TILE)
        r = np.arange(tq)
        q_tiles = [tq + r] * 2
        k_tiles = [tq + r, r]
    else:
        r = P + np.arange(T)
        new = np.where(np.arange(ATT_TILE) < T, P + np.arange(ATT_TILE), -1)
        q_tiles = [r, r]
        k_tiles = [P - ATT_TILE + np.arange(ATT_TILE), new]
    return _bias_tiles(rel_bias, q_tiles, k_tiles)


def _trunk(x, k_hist, v_hist, s_hist, pool_hist, conv_hist, w, diff_lambda, rel_bias, time_major_ffn):
    B, T, D = x.shape
    P = 0 if k_hist is None else k_hist.shape[2]
    bias = _attention_bias(rel_bias, T, P)
    hist = None
    if k_hist is not None:
        hist = (k_hist.reshape(N_MIX, B, P * HEADS, HEAD_W), v_hist.reshape(N_MIX, B, P * HEADS, HEAD_W), s_hist)
    kv = None
    xb = x.astype(BF16)
    ss, ps, cs = [], [], []
    for l in range(DEPTH):
        if l % 2 == 0:
            m = l // 2
            lam_init = 0.8 - 0.6 * math.exp(-0.3 * l)
            oa, ob, k_all, v_all, s_new = _mixer(
                xb, w["w7"][m], w["lb_logits"], w["a_gain"][m], w["b_gain"][m], diff_lambda[m].astype(F32),
                bias, hist, kv, m, lam_init)
            kv = (k_all, v_all)
            ss.append(s_new)
            pre = ("proj", oa, ob, w["w_out"][m], w["ln1_g"][l], w["ln1_b"][l])
        else:
            p = l // 2
            pre = ("pool", pool_hist[p], w["pool_w"][p], w["pool_scale"][p], w["ln1_g"][l], w["ln1_b"][l], P)
        ffn_w = (w["w_up"][l], w["conv_w"][l], w["conv_b"][l], w["w_down"][l], w["ln2_g"][l], w["ln2_b"][l])
        want_bf16 = (l + 1 < DEPTH) and (l + 1) % 2 == 0
        if time_major_ffn:
            if pre[0] == "proj":
                x = _out_proj_ln(x.reshape(B * T, D), oa.reshape(B * T, A_W), ob.reshape(B * T, A_W),
                                 *pre[3:]).reshape(B, T, D)
            else:
                x, ph = _pool_ln(x, *pre[1:])
                ps.append(ph)
            xt = x.transpose(1, 0, 2).reshape(1, T * B, D)
            ht = conv_hist[l].transpose(1, 0, 2).reshape(1, (CONV_W - 1) * B, D_FF)
            outs, ch = _conv_ffn_ln(xt, ht, *ffn_w, sh=B, emit_bf16=want_bf16)
            outs = [o.reshape(T, B, D).transpose(1, 0, 2) for o in outs]
            ch = ch.reshape(CONV_W - 1, B, D_FF).transpose(1, 0, 2)
        else:
            outs, ch = _conv_ffn_ln(x, conv_hist[l], *ffn_w, sh=1, emit_bf16=want_bf16, pre=pre)
            if pre[0] == "pool":
                ps.append(outs[-1])
        x = outs[0]
        if want_bf16:
            xb = outs[1]
        cs.append(ch)
    k_all = kv[0].reshape(N_MIX, B, T, HEADS, HEAD_W)
    v_all = kv[1].reshape(N_MIX, B, T, HEADS, HEAD_W)
    return x, k_all, v_all, jnp.stack(ss), jnp.stack(ps), jnp.stack(cs)


def kernel(x_prompt, x_sample, cache_k, cache_v, state_hgrn, state_pool, state_ffn_conv, w_in_ab, w_out_ab,
           lb_logits, hgrn_norm_g, diff_lambda, diff_norm_g, rel_bias, pool_w, pool_scale, ffn_w_up,
           ffn_conv_w, ffn_conv_b, ffn_w_down, ln1_g, ln1_b, ln2_g, ln2_b):
    w = _prep_weights(w_in_ab, w_out_ab, lb_logits, hgrn_norm_g, diff_norm_g, pool_w, pool_scale, ffn_w_up,
                      ffn_conv_w, ffn_conv_b, ffn_w_down, ln1_g, ln1_b, ln2_g, ln2_b)
    Bp, Tp, _ = x_prompt.shape
    dt = x_prompt.dtype
    y_p, k_p, v_p, s_p, pool_p, conv_p = _trunk(
        x_prompt, None, None, None,
        jnp.zeros((DEPTH // 2, Bp, POOL_HIST, D_MODEL), dt),
        jnp.zeros((DEPTH, Bp, CONV_W - 1, D_FF), dt),
        w, diff_lambda, rel_bias, time_major_ffn=False)
    y_s, k_s, v_s, s_s, pool_s, conv_s = _trunk(
        x_sample, cache_k, cache_v, state_hgrn, state_pool, state_ffn_conv,
        w, diff_lambda, rel_bias, time_major_ffn=True)
    return (y_p, y_s, k_p, v_p, s_p, pool_p, conv_p, k_s, v_s, s_s, pool_s, conv_s)
```

```python
import functools
import math

import numpy as np
import jax
import jax.numpy as jnp
from jax import lax
from jax.experimental import pallas as pl
from jax.experimental.pallas import tpu as pltpu

F32 = jnp.float32
BF16 = jnp.bfloat16

D_MODEL = 1024
DEPTH = 4
CHUNK = 64
N_MIX = (DEPTH + 1) // 2
HEADS = 4
HEAD_W = 128
A_W = HEADS * HEAD_W
B_DQK = 64
POOL_WINDOWS = (2, 4, 8, 16)
POOL_GROUP = D_MODEL // len(POOL_WINDOWS)
POOL_HIST = max(POOL_WINDOWS) - 1
D_FF = 256 * math.ceil(8 * D_MODEL / 3 / 256)
CONV_W = 3
NUM_BUCKETS = 32
MAX_DISTANCE = 128
DN_ALPHA = (2 * DEPTH) ** 0.25
LN_EPS = 1e-5
RMS_EPS = 1e-6
MASK_NEG = -1e30
LOG2E = 1.0 / math.log(2.0)

VMEM_LIMIT_BYTES = 56 * 1024 * 1024
ATT_TILE = 256
ROW_TILE = 512
FFN_ROW_TILE = 1024
FFN_SUB_TILE = 512
FF_CHUNK = 256
N_PROJ = 7
HGRN_SUB = 4


def _nt(a, b):
    return lax.dot_general(a, b, (((1,), (1,)), ((), ())), preferred_element_type=F32)


def _tn(a, b):
    return lax.dot_general(a, b, (((0,), (0,)), ((), ())), preferred_element_type=F32)


def _mm(a, b):
    return jnp.dot(a, b, preferred_element_type=F32)


def _layer_norm(y, g, b):
    mu = jnp.mean(y, axis=-1, keepdims=True)
    yc = y - mu
    var = jnp.mean(yc * yc, axis=-1, keepdims=True)
    return yc * lax.rsqrt(var + LN_EPS) * g + b


def _rms_norm(y, g):
    return y * lax.rsqrt(jnp.mean(y * y, axis=-1, keepdims=True) + RMS_EPS) * g


def _compiler_params(semantics):
    return pltpu.CompilerParams(dimension_semantics=semantics, vmem_limit_bytes=VMEM_LIMIT_BYTES)


def _rel_bucket(rel):
    half = NUM_BUCKETS // 2
    max_exact = half // 2
    base = jnp.where(rel > 0, half, 0)
    n = jnp.abs(rel)
    nf = jnp.maximum(n, 1).astype(jnp.float32)
    large = max_exact + (jnp.log(nf / max_exact) / math.log(MAX_DISTANCE / max_exact)
                         * (half - max_exact)).astype(jnp.int32)
    large = jnp.minimum(large, half - 1)
    return base + jnp.where(n < max_exact, n, large)


def _bias_tile_body(far_bucket, bucket_ref, rb_ref, out_ref):
    h = pl.program_id(0)
    bucket = bucket_ref[...]
    acc = jnp.zeros(bucket.shape, F32)
    for b in range(NUM_BUCKETS):
        acc = jnp.where(bucket == b, rb_ref[b, h], acc)
    out_ref[...] = jnp.where(bucket < 0, MASK_NEG, (acc - rb_ref[far_bucket, h]) * LOG2E)


def _bias_tiles(rel_bias, q_pos_tiles, k_pos_tiles):
    buckets = []
    for qp, kp in zip(q_pos_tiles, k_pos_tiles):
        qp = jnp.asarray(qp, jnp.int32)
        kp = jnp.asarray(kp, jnp.int32)
        visible = (kp[None, :] // CHUNK) <= (qp[:, None] // CHUNK)
        buckets.append(jnp.where(visible & (kp[None, :] >= 0), _rel_bucket(kp[None, :] - qp[:, None]), -1))
    bucket = jnp.stack(buckets).astype(jnp.int32)
    n, r, c = bucket.shape
    far_bucket = NUM_BUCKETS // 2 - 1
    return pl.pallas_call(
        functools.partial(_bias_tile_body, far_bucket),
        grid=(HEADS, n),
        in_specs=[pl.BlockSpec((None, r, c), lambda h, i: (i, 0, 0)),
                  pl.BlockSpec(memory_space=pltpu.SMEM)],
        out_specs=pl.BlockSpec((None, None, r, c), lambda h, i: (h, i, 0, 0)),
        out_shape=jax.ShapeDtypeStruct((HEADS, n, r, c), F32),
        name="bias_tiles",
    )(bucket, rel_bias.astype(F32))


def _hgrn_constants(L, n_sub):
    levels = []
    w = L // 2
    while w >= 1:
        levels.append(w)
        w //= 2
    t = np.arange(L)[:, None]
    u = np.arange(L)[None, :]
    blocks = [(u <= t)]
    for w in levels:
        start = (t // w) * w
        blocks.append((u > start) & (u <= t))
        nxt = np.minimum((t // w + 1) * w, L - 1)
        blocks.append((u > t) & (u <= nxt))
    g = np.concatenate(blocks, axis=0).astype(np.float32)
    g3 = np.concatenate([g, g, g], axis=1)
    level = np.full((L, L), -1, np.int32)
    s = u
    for i, w in enumerate(levels):
        hit = (t // (2 * w) == s // (2 * w)) & ((t // w) % 2 == 1) & ((s // w) % 2 == 0)
        level[hit] = i
    level[np.arange(L), np.arange(L)] = len(levels)
    pair = np.concatenate([level] * min(n_sub, 2), axis=1)
    return jnp.asarray(g3, BF16), jnp.asarray(pair, jnp.int32), len(levels)


def _diff_softmax_pv(s1, s2, lam, values):
    def probs(ss):
        m = functools.reduce(jnp.maximum, [jnp.max(s, axis=-1, keepdims=True) for s in ss])
        ps = [jnp.exp2(s - m) for s in ss]
        l = functools.reduce(jnp.add, [jnp.sum(p, axis=-1, keepdims=True) for p in ps])
        return ps, l
    p1, l1 = probs(s1)
    p2, l2 = probs(s2)
    c1 = 1.0 / l1
    c2 = lam / l2
    return functools.reduce(jnp.add, [_mm((c1 * a - c2 * b).astype(BF16), v) for a, b, v in zip(p1, p2, values)])


def _mixer_body(cfg, *refs):
    P, m_idx = cfg[1], cfg[4]
    if P == 0:
        _mixer_head(cfg, pl.program_id(1), *refs)
        return
    refs = list(refs)
    n_in = 12 + (2 if m_idx > 0 else 0)
    for hh in range(HEADS):
        cols = pl.ds(hh * HEAD_W, HEAD_W)
        view = list(refs)
        for pos in (1, 2, 3, 6, 11, n_in + 4):
            view[pos] = refs[pos].at[hh]
        for pos in (n_in, n_in + 1):
            view[pos] = refs[pos].at[:, cols]
        _mixer_head(cfg, hh, *view)


def _mixer_head(cfg, h, *refs):
    T, P, L, n_sub, m_idx, lam_init, n_levels = cfg
    has_hist = P > 0
    R = n_sub * L
    (x_ref, w_ref, lbl_ref, ag_ref, bg_ref, dl_ref, bias_ref, g_ref, lev_ref), refs = refs[:9], refs[9:]
    if has_hist:
        (hk_ref, hv_ref, s0_ref), refs = refs[:3], refs[3:]
    if m_idx > 0:
        refs = refs[2:]
    (oa_ref, ob_ref, k_ref, v_ref, s_ref, proj_ref, st_ref, q1_ref, q2_ref, kb_ref, vb_ref), refs = refs[:11], refs[11:]
    n_tiles = T // R
    lane = lax.broadcasted_iota(jnp.int32, (1, HEAD_W), 1)

    def run(jobs):
        jobs = list(jobs)
        while jobs:
            for job in list(jobs):
                try:
                    next(job)
                except StopIteration:
                    jobs.remove(job)

    def project(i):
        rows = slice(i * R, (i + 1) * R)
        for c0 in range(0, N_PROJ * HEAD_W, 2 * HEAD_W):
            c1 = min(c0 + 2 * HEAD_W, N_PROJ * HEAD_W)
            proj = _mm(x_ref[rows, :], w_ref[:, c0:c1])
            proj_ref[rows, c0:c1] = proj
            for k in range(c0 // HEAD_W, c1 // HEAD_W):
                blk = proj[:, k * HEAD_W - c0:(k + 1) * HEAD_W - c0]
                if k == 4:
                    qs = blk * (B_DQK ** -0.5 * LOG2E)
                    q1_ref[rows, :] = jnp.where(lane < B_DQK, qs, 0.0).astype(BF16)
                    q2_ref[rows, :] = jnp.where(lane >= B_DQK, qs, 0.0).astype(BF16)
                elif k == 5:
                    kb_ref[rows, :] = blk.astype(BF16)
                elif k == 6:
                    vb_ref[rows, :] = blk.astype(BF16)
            yield

    lg = lbl_ref[...]
    e = jnp.exp(lg - jnp.max(lg, axis=0, keepdims=True))
    sm = e / jnp.sum(e, axis=0, keepdims=True)
    lb = jnp.sum(sm[0:m_idx + 1], axis=0, keepdims=True) - sm[0:1]
    a_gain = ag_ref[...]

    if has_hist:
        st_ref[...] = s0_ref[...].T
    else:
        st_ref[...] = jnp.zeros((HEAD_W, HEAD_W), F32)
    level = lev_ref[...]

    def lanes(j, n=1):
        return slice(j * HEAD_W, (j + n) * HEAD_W)

    def side_by_side(parts):
        return parts[0] if len(parts) == 1 else jnp.concatenate(parts, axis=1)

    lb_w = side_by_side([lb] * n_sub)
    zero_blk = jnp.zeros((L, HEAD_W), BF16)
    groups = [list(range(j, min(j + 2, n_sub))) for j in range(0, n_sub, 2)]

    def block_diag(a, js):
        if len(js) == 1:
            return a[:, lanes(js[0])]
        return jnp.concatenate(
            [side_by_side([a[:, lanes(j)] if jj == j else zero_blk for jj in js]) for j in js], axis=0)

    def hgrn_step(c):
        def stream(k):
            return side_by_side([proj_ref[c * R + j * L:c * R + (j + 1) * L, lanes(k)] for j in range(n_sub)])

        q = stream(0)
        zf = stream(1)
        vi = stream(2).astype(BF16)
        gate = stream(3)
        log_f = jnp.log(lb_w + (1.0 - lb_w) * jax.nn.sigmoid(zf))
        k_in = (1.0 - lb_w) * jax.nn.sigmoid(-zf)
        hi = log_f.astype(BF16)
        r1 = log_f - hi.astype(F32)
        mid = r1.astype(BF16)
        lo = (r1 - mid.astype(F32)).astype(BF16)
        ex = _mm(g_ref[...], jnp.concatenate([hi, mid, lo], axis=0))
        b = ex[0:L]
        yield

        def pair_scores(qw, kw, js):
            return _nt(qw[:, lanes(js[0], len(js))], block_diag(kw, js))

        scores = [jnp.zeros((L, len(js) * L), F32) for js in groups]
        for i in range(n_levels + 1):
            if i < n_levels:
                qw = (q * jnp.exp(ex[(1 + 2 * i) * L:(2 + 2 * i) * L])).astype(BF16)
                kw = (k_in * jnp.exp(ex[(2 + 2 * i) * L:(3 + 2 * i) * L])).astype(BF16)
            else:
                qw, kw = q.astype(BF16), k_in.astype(BF16)
            for gi, js in enumerate(groups):
                lev = level if len(js) == level.shape[1] // L else level[:, 0:len(js) * L]
                scores[gi] = jnp.where(lev == i, pair_scores(qw, kw, js), scores[gi])
            if i % 2 == 1:
                yield
        o = _mm(side_by_side(scores).astype(BF16), block_diag(vi, list(range(n_sub))))
        qd = (q * jnp.exp(b)).astype(BF16)
        st = st_ref[...]
        o_state = []
        for j in range(n_sub):
            o_state.append(_nt(qd[:, lanes(j)], st.astype(BF16)))
            b_last = b[L - 1:L, lanes(j)]
            kd = (k_in[:, lanes(j)] * jnp.exp(b_last - b[:, lanes(j)])).astype(BF16)
            st = st * jnp.exp(b_last) + _tn(vi[:, lanes(j)], kd)
            if j % 2 == 1:
                yield
        st_ref[...] = st
        o = o + side_by_side(o_state)
        act = gate * jax.nn.sigmoid(gate)
        for j in range(n_sub):
            oa_ref[c * R + j * L:c * R + (j + 1) * L, :] = (
                _rms_norm(o[:, lanes(j)], a_gain) * act[:, lanes(j)]).astype(BF16)

    lp = dl_ref[...]
    lam = (jnp.exp(jnp.sum(lp[0:1] * lp[1:2], axis=-1, keepdims=True))
           - jnp.exp(jnp.sum(lp[2:3] * lp[3:4], axis=-1, keepdims=True)) + lam_init)
    b_gain = bg_ref[...] * (1.0 - lam_init)

    def attend(i):
        segs = []
        if i >= 2:
            segs.append((0, (i - 1) * R, None))
        if i >= 1:
            segs.append(((i - 1) * R, i * R, 1))
        segs.append((i * R, (i + 1) * R, 0))

        rows = slice(i * R, (i + 1) * R)
        stats = []
        for q_ref in (q1_ref, q2_ref):
            q = q_ref[rows, :]
            ss = []
            for lo_k, hi_k, kind in segs:
                s = _nt(q, kb_ref[lo_k:hi_k, :])
                ss.append(s if kind is None else s + bias_ref[kind])
            m = functools.reduce(jnp.maximum, [jnp.max(s, axis=-1, keepdims=True) for s in ss])
            yield
            ps = []
            for s in ss:
                ps.append(jnp.exp2(s - m))
                yield
            stats.append((ps, functools.reduce(jnp.add, [jnp.sum(p, axis=-1, keepdims=True) for p in ps])))
        (p1, l1), (p2, l2) = stats
        c1 = 1.0 / l1
        c2 = lam / l2
        o = None
        for a, b, (lo_k, hi_k, _) in zip(p1, p2, segs):
            part = _mm((c1 * a - c2 * b).astype(BF16), vb_ref[lo_k:hi_k, :])
            o = part if o is None else o + part
            yield
        ob_ref[rows, :] = (_rms_norm(o, 1.0) * b_gain).astype(BF16)

    def store_kv(hh):
        k_ref[pl.ds(hh, T, stride=HEADS), :] = proj_ref[:, 5 * HEAD_W:6 * HEAD_W]
        v_ref[pl.ds(hh, T, stride=HEADS), :] = proj_ref[:, 6 * HEAD_W:7 * HEAD_W]

    if has_hist:
        kd_ref, vd_ref = refs
        kb_ref[...] = jnp.zeros(kb_ref.shape, BF16)
        vb_ref[...] = jnp.zeros(vb_ref.shape, BF16)

        def gather_cache():
            piece = P // 4
            for src, dst in ((hk_ref, kd_ref), (hv_ref, vd_ref)):
                for r0 in range(0, P, piece):
                    dst[r0:r0 + piece, :] = src[pl.ds(h + r0 * HEADS, piece, stride=HEADS), :].astype(BF16)
                    yield

    run([project(0)])
    for i in range(n_tiles):
        jobs = [project(i + 1)] if i + 1 < n_tiles else []
        jobs.append(hgrn_step(i))
        jobs.append(gather_cache() if has_hist else attend(i))
        run(jobs)
    s_ref[...] = st_ref[...].T

    if has_hist:
        store_kv(h)
    else:
        for hh in range(HEADS):
            pl.when(h == hh)(functools.partial(store_kv, hh))

    if has_hist:
        tk = ATT_TILE
        bias_new = bias_ref[1, :, 0:HEAD_W]

        def scores_of(q):
            return [_nt(q, kd_ref[0:P - tk, :]), _nt(q, kd_ref[P - tk:P, :]) + bias_ref[0],
                    _nt(q, kb_ref[...]) + bias_new]

        o = _diff_softmax_pv(scores_of(q1_ref[...]), scores_of(q2_ref[...]), lam,
                             [vd_ref[0:P - tk, :], vd_ref[P - tk:P, :], vb_ref[...]])
        ob_ref[...] = (_rms_norm(o, 1.0) * b_gain).astype(BF16)


def _mixer(x, w7, lb_logits, a_gain, b_gain, dlam, bias, hist, kv_prev, m_idx, lam_init):
    B, T, D = x.shape
    L = min(T, CHUNK)
    n_sub = min(T // L, HGRN_SUB)
    gmat, level, n_levels = _hgrn_constants(L, n_sub)
    P = 0 if hist is None else hist[0].shape[2] // HEADS
    assert T % (n_sub * L) == 0 and T % min(T, ROW_TILE) == 0 and T % min(T, ATT_TILE) == 0
    assert ATT_TILE > MAX_DISTANCE and (T <= HEAD_W if hist is not None else True)
    assert P % ATT_TILE == 0 and (P == 0 or P >= 2 * ATT_TILE)
    assert hist is not None or n_sub * L == min(T, ATT_TILE)
    cfg = (T, P, L, n_sub, m_idx, lam_init, n_levels)
    wcols = N_PROJ * HEAD_W
    per_head = hist is None
    hd = None if per_head else HEADS
    hw = HEAD_W if per_head else A_W
    grid = (B, HEADS if per_head else 1)
    assert per_head or T == n_sub * L
    in_specs = [
        pl.BlockSpec((None, T, D), lambda b, h: (b, 0, 0)),
        pl.BlockSpec((hd, D, wcols), lambda b, h: (h, 0, 0)),
        pl.BlockSpec((hd, N_MIX, HEAD_W), lambda b, h: (h, 0, 0)),
        pl.BlockSpec((hd, 1, HEAD_W), lambda b, h: (h, 0, 0)),
        pl.BlockSpec((1, HEAD_W), lambda b, h: (0, 0)),
        pl.BlockSpec(dlam.shape, lambda b, h: (0, 0)),
        pl.BlockSpec((hd,) + bias.shape[1:], lambda b, h: (h, 0, 0, 0)),
        pl.BlockSpec(gmat.shape, lambda b, h: (0, 0)),
        pl.BlockSpec(level.shape, lambda b, h: (0, 0)),
    ]
    args = [x, w7, lb_logits, a_gain, b_gain, dlam, bias, gmat, level]
    if hist is not None:
        hk, hv, s0 = hist
        in_specs += [
            pl.BlockSpec((None, None, P * HEADS, HEAD_W), lambda b, h: (m_idx, b, 0, 0)),
            pl.BlockSpec((None, None, P * HEADS, HEAD_W), lambda b, h: (m_idx, b, 0, 0)),
            pl.BlockSpec((None, None, HEADS, HEAD_W, HEAD_W), lambda b, h: (m_idx, b, 0, 0, 0)),
        ]
        args += [hk, hv, s0]
    aliases = {}
    if m_idx > 0:
        aliases = {len(args): 2, len(args) + 1: 3}
        in_specs += [pl.BlockSpec(memory_space=pl.ANY), pl.BlockSpec(memory_space=pl.ANY)]
        args += list(kv_prev)
    col_spec = pl.BlockSpec((None, T, hw), lambda b, h: (b, 0, h))
    kv_spec = pl.BlockSpec((None, None, T * HEADS, HEAD_W), lambda b, h: (m_idx, b, 0, 0))
    kv_shape = jax.ShapeDtypeStruct((N_MIX, B, T * HEADS, HEAD_W), F32)
    key_rows = T if hist is None else HEAD_W
    scratch = [pltpu.VMEM((T, wcols), F32), pltpu.VMEM((HEAD_W, HEAD_W), F32),
               pltpu.VMEM((T, HEAD_W), BF16), pltpu.VMEM((T, HEAD_W), BF16),
               pltpu.VMEM((key_rows, HEAD_W), BF16), pltpu.VMEM((key_rows, HEAD_W), BF16)]
    if hist is not None:
        scratch += [pltpu.VMEM((P, HEAD_W), BF16), pltpu.VMEM((P, HEAD_W), BF16)]
    return pl.pallas_call(
        functools.partial(_mixer_body, cfg),
        grid=grid,
        in_specs=in_specs,
        out_specs=[col_spec, col_spec, kv_spec, kv_spec,
                   pl.BlockSpec((None, hd, HEAD_W, HEAD_W), lambda b, h: (b, h, 0, 0))],
        out_shape=[jax.ShapeDtypeStruct((B, T, A_W), BF16), jax.ShapeDtypeStruct((B, T, A_W), BF16),
                   kv_shape, kv_shape, jax.ShapeDtypeStruct((B, HEADS, HEAD_W, HEAD_W), F32)],
        scratch_shapes=scratch,
        input_output_aliases=aliases,
        compiler_params=_compiler_params(("arbitrary", "arbitrary")),
        name="mixer_ab",
    )(*args)


def _out_proj_body(x_ref, oa_ref, ob_ref, w_ref, g_ref, b_ref, out_ref):
    y = _mm(oa_ref[...], w_ref[0:A_W, :]) + _mm(ob_ref[...], w_ref[A_W:2 * A_W, :])
    out_ref[...] = _layer_norm(DN_ALPHA * x_ref[...] + y, g_ref[...], b_ref[...])


def _out_proj_ln(x, oa, ob, w_out, g, b):
    N, D = x.shape
    tm = min(N, ROW_TILE)
    row = lambda w: pl.BlockSpec((tm, w), lambda i: (i, 0))
    full = lambda a: pl.BlockSpec(a.shape, lambda i: (0,) * a.ndim)
    return pl.pallas_call(
        _out_proj_body,
        grid=(N // tm,),
        in_specs=[row(D), row(A_W), row(A_W), full(w_out), full(g), full(b)],
        out_specs=row(D),
        out_shape=jax.ShapeDtypeStruct((N, D), F32),
        compiler_params=_compiler_params(("arbitrary",)),
        name="out_proj_ln",
    )(x, oa, ob, w_out, g, b)


def _pool_body(cfg, x_ref, prev_ref, hist_ref, pw_ref, sc_ref, g_ref, b_ref, out_ref, ph_ref, xs_ref):
    tm, pos0 = cfg
    H = POOL_HIST + 1
    i = pl.program_id(1)
    x = x_ref[...]
    xs_ref[H:H + tm, :] = x

    @pl.when(i == 0)
    def _():
        xs_ref[0:1, :] = jnp.zeros((1, D_MODEL), F32)
        xs_ref[1:H, :] = hist_ref[...]

    @pl.when(i > 0)
    def _():
        xs_ref[0:H, :] = prev_ref[...]

    pos = pos0 + i * tm + lax.broadcasted_iota(jnp.int32, (tm, 1), 0)
    ys = []
    for g, w in enumerate(POOL_WINDOWS):
        cols = slice(g * POOL_GROUP, (g + 1) * POOL_GROUP)
        s = x[:, cols]
        for k in range(1, w):
            s = s + xs_ref[H - k:H - k + tm, cols]
        cnt = jnp.minimum(w, pos + 1).astype(F32)
        pooled = s / cnt - x[:, cols]
        ys.append(_mm(pooled.astype(BF16), pw_ref[g]))
    y = jnp.concatenate(ys, axis=-1) * sc_ref[...]
    out_ref[...] = _layer_norm(DN_ALPHA * x + y, g_ref[...], b_ref[...])
    ph_ref[...] = xs_ref[tm + 1:tm + H, :]


def _pool_ln(x, hist, pool_w, scale, g, b, pos0):
    B, T, D = x.shape
    H = POOL_HIST + 1
    tm = min(T, ROW_TILE)
    per = tm // H
    full = lambda a: pl.BlockSpec(a.shape, lambda bi, i: (0,) * a.ndim)
    return pl.pallas_call(
        functools.partial(_pool_body, (tm, pos0)),
        grid=(B, T // tm),
        in_specs=[pl.BlockSpec((None, tm, D), lambda bi, i: (bi, i, 0)),
                  pl.BlockSpec((None, H, D), lambda bi, i: (bi, jnp.maximum(i * per - 1, 0), 0)),
                  pl.BlockSpec((None, POOL_HIST, D), lambda bi, i: (bi, 0, 0)),
                  full(pool_w), full(scale), full(g), full(b)],
        out_specs=[pl.BlockSpec((None, tm, D), lambda bi, i: (bi, i, 0)),
                   pl.BlockSpec((None, POOL_HIST, D), lambda bi, i: (bi, 0, 0))],
        out_shape=[jax.ShapeDtypeStruct((B, T, D), F32), jax.ShapeDtypeStruct((B, POOL_HIST, D), F32)],
        scratch_shapes=[pltpu.VMEM((H + tm, D), F32)],
        compiler_params=_compiler_params(("arbitrary", "arbitrary")),
        name="pool_ln",
    )(x, x, hist, pool_w, scale, g, b)


def _pooled_rows(x, halo, pos, pw_ref, scale):
    H = POOL_HIST + 1
    ys = []
    for g, w in enumerate(POOL_WINDOWS):
        cols = slice(g * POOL_GROUP, (g + 1) * POOL_GROUP)
        xg = x[:, cols]
        sg = xg
        eg = jnp.concatenate([halo[:, cols], xg[0:H]], axis=0)
        d = 1
        while d < w:
            sg = sg + pltpu.roll(sg, d, 0)
            eg = eg + pltpu.roll(eg, d, 0)
            d *= 2
        sg = jnp.concatenate([eg[H:2 * H], sg[H:]], axis=0)
        cnt = jnp.minimum(w, pos + 1).astype(F32)
        pooled = sg / cnt - xg
        ys.append(_mm(pooled.astype(BF16), pw_ref[g]))
    return jnp.concatenate(ys, axis=-1) * scale


def _ffn_body(cfg, *refs):
    tm, sh, emit_bf16, pre, pos0 = cfg
    x_ref, hist_ref, wup_ref, cw_ref, cb_ref, wdn_ref, g_ref, b_ref = refs[:8]
    refs = refs[8:]
    if pre == "proj":
        (oa_ref, ob_ref, wo_ref, g1_ref, b1_ref), refs = refs[:5], refs[5:]
    elif pre == "pool":
        (prev_ref, ph_in_ref, pw_ref, sc_ref, g1_ref, b1_ref), refs = refs[:6], refs[6:]
    out_ref, refs = refs[0], refs[1:]
    if emit_bf16:
        outb_ref, refs = refs[0], refs[1:]
    if pre == "pool":
        ph_ref, refs = refs[0], refs[1:]
    cs_ref, carry_ref, gbuf_ref = refs
    halo = (CONV_W - 1) * sh
    pad = -(-halo // 8) * 8
    top = max(pad, 16)
    n_chunks = D_FF // FF_CHUNK
    H = POOL_HIST + 1
    i = pl.program_id(1)

    @pl.when(i == 0)
    def _():
        carry_ref[...] = hist_ref[...]

    def conv_act(a_cur, shifted, gate, cols):
        conv = cb_ref[:, cols]
        for j in range(CONV_W - 1):
            conv = conv + shifted(j) * cw_ref[j:j + 1, cols]
        conv = conv + a_cur * cw_ref[CONV_W - 1:CONV_W, cols]
        return (conv * jax.nn.sigmoid(conv) * gate).astype(BF16)

    n_sub = tm // FFN_SUB_TILE if tm % FFN_SUB_TILE == 0 else 1
    ts = tm // n_sub
    for s in range(n_sub):
        rows = slice(s * ts, (s + 1) * ts)
        x = x_ref[rows, :]
        if pre == "proj":
            y = _mm(oa_ref[rows, :], wo_ref[0:A_W, :]) + _mm(ob_ref[rows, :], wo_ref[A_W:2 * A_W, :])
            x = _layer_norm(DN_ALPHA * x + y, g1_ref[...], b1_ref[...])
        elif pre == "pool":
            if s == 0:
                before = jnp.where(i == 0, ph_in_ref[...], prev_ref[...])
            else:
                before = x_ref[s * ts - H:s * ts, :]
            pos = pos0 + i * tm + s * ts + lax.broadcasted_iota(jnp.int32, (ts, 1), 0)
            y = _pooled_rows(x, before, pos, pw_ref, sc_ref[...])
            x = _layer_norm(DN_ALPHA * x + y, g1_ref[...], b1_ref[...])
        xb = x.astype(BF16)
        for c in range(n_chunks):
            cols = slice(c * FF_CHUNK, (c + 1) * FF_CHUNK)
            a = _mm(xb, wup_ref[:, cols])
            gate = _mm(xb, wup_ref[:, D_FF + c * FF_CHUNK:D_FF + (c + 1) * FF_CHUNK])
            gbuf_ref[rows, cols] = conv_act(a, lambda j: pltpu.roll(a, (CONV_W - 1 - j) * sh, 0), gate, cols)
            ext = jnp.concatenate([carry_ref[:, cols], a[0:top]], axis=0)
            gbuf_ref[s * ts:s * ts + top, cols] = conv_act(
                a[0:top], lambda j: pltpu.roll(ext, (CONV_W - 1 - j) * sh, 0)[pad:pad + top], gate[0:top], cols)
            carry_ref[:, cols] = a[ts - pad:ts]
        f = _mm(gbuf_ref[rows, :], wdn_ref[...])
        y = _layer_norm(DN_ALPHA * x + f, g_ref[...], b_ref[...])
        out_ref[rows, :] = y
        if emit_bf16:
            outb_ref[rows, :] = y.astype(BF16)
    cs_ref[...] = carry_ref[...]
    if pre == "pool":
        ph_ref[...] = x_ref[tm - POOL_HIST:tm, :]


def _conv_ffn_ln(x, hist, w_up, conv_w, conv_b, w_down, g, b, sh, emit_bf16, pre=None):
    B, T, D = x.shape
    halo = (CONV_W - 1) * sh
    tm = min(T, FFN_ROW_TILE)
    pad = -(-halo // 8) * 8
    H = POOL_HIST + 1
    assert T % tm == 0 and tm >= max(pad, 16)
    hist_p = jnp.pad(hist, ((0, 0), (pad - halo, 0), (0, 0)))
    full = lambda a: pl.BlockSpec(a.shape, lambda bi, i: (0,) * a.ndim)
    row = lambda w: pl.BlockSpec((None, tm, w), lambda bi, i: (bi, i, 0))
    hist_spec = pl.BlockSpec((None, pad, D_FF), lambda bi, i: (bi, 0, 0))
    kind, pos0, pre_args, pre_specs = None, 0, [], []
    if pre is not None:
        kind = pre[0]
        if kind == "proj":
            _, oa, ob, w_out, g1, b1 = pre
            pre_args = [oa, ob, w_out, g1, b1]
            pre_specs = [row(A_W), row(A_W), full(w_out), full(g1), full(b1)]
        else:
            _, pool_hist, pool_w, scale, g1, b1, pos0 = pre
            assert sh == 1 and tm % H == 0
            ph_in = jnp.pad(pool_hist, ((0, 0), (1, 0), (0, 0)))
            pre_args = [x, ph_in, pool_w, scale, g1, b1]
            pre_specs = [pl.BlockSpec((None, H, D), lambda bi, i: (bi, jnp.maximum(i * (tm // H) - 1, 0), 0)),
                         pl.BlockSpec((None, H, D), lambda bi, i: (bi, 0, 0)),
                         full(pool_w), full(scale), full(g1), full(b1)]
    out_specs = [row(D)] + ([row(D)] if emit_bf16 else [])
    out_shape = [jax.ShapeDtypeStruct((B, T, D), F32)] + ([jax.ShapeDtypeStruct((B, T, D), BF16)] if emit_bf16 else [])
    if kind == "pool":
        out_specs.append(pl.BlockSpec((None, POOL_HIST, D), lambda bi, i: (bi, 0, 0)))
        out_shape.append(jax.ShapeDtypeStruct((B, POOL_HIST, D), F32))
    out_specs.append(hist_spec)
    out_shape.append(jax.ShapeDtypeStruct((B, pad, D_FF), F32))
    res = pl.pallas_call(
        functools.partial(_ffn_body, (tm, sh, emit_bf16, kind, pos0)),
        grid=(B, T // tm),
        in_specs=[row(D), hist_spec,
                  full(w_up), full(conv_w), full(conv_b), full(w_down), full(g), full(b)] + pre_specs,
        out_specs=out_specs,
        out_shape=out_shape,
        scratch_shapes=[pltpu.VMEM((pad, D_FF), F32), pltpu.VMEM((tm, D_FF), BF16)],
        compiler_params=_compiler_params(("arbitrary", "arbitrary")),
        name="conv_ffn_ln",
    )(x, hist_p, w_up, conv_w, conv_b, w_down, g, b, *pre_args)
    return res[:-1], res[-1][:, pad - halo:, :]


def _prep_weights(w_in_ab, w_out_ab, lb_logits, hgrn_norm_g, diff_norm_g, pool_w, pool_scale, ffn_w_up,
                  ffn_conv_w, ffn_conv_b, ffn_w_down, ln1_g, ln1_b, ln2_g, ln2_b):
    w = {}
    w["w7"] = (w_in_ab.astype(BF16).reshape(N_MIX, D_MODEL, N_PROJ, HEADS, HEAD_W)
               .transpose(0, 3, 1, 2, 4).reshape(N_MIX, HEADS, D_MODEL, N_PROJ * HEAD_W))
    w["w_out"] = w_out_ab.astype(BF16)
    w["lb_logits"] = lb_logits.astype(F32).reshape(N_MIX, HEADS, HEAD_W).transpose(1, 0, 2)
    w["a_gain"] = hgrn_norm_g.astype(F32).reshape(N_MIX, HEADS, 1, HEAD_W)
    w["b_gain"] = diff_norm_g.astype(F32).reshape(N_MIX, 1, HEAD_W)
    w["pool_w"] = pool_w.astype(BF16)
    w["pool_scale"] = pool_scale.astype(F32).reshape(-1, 1, D_MODEL)
    w["w_up"] = ffn_w_up.astype(BF16)
    w["conv_w"] = ffn_conv_w.astype(F32)
    w["conv_b"] = ffn_conv_b.astype(F32).reshape(DEPTH, 1, D_FF)
    w["w_down"] = ffn_w_down.astype(BF16)
    for name, v in (("ln1_g", ln1_g), ("ln1_b", ln1_b), ("ln2_g", ln2_g), ("ln2_b", ln2_b)):
        w[name] = v.astype(F32).reshape(DEPTH, 1, D_MODEL)
    return w


def _attention_bias(rel_bias, T, P):
    if P == 0:
        tq = min(T, ATT_TILE)
        r = np.arange(tq)
        q_tiles = [tq + r] * 2
        k_tiles = [tq + r, r]
    else:
        r = P + np.arange(T)
        new = np.where(np.arange(ATT_TILE) < T, P + np.arange(ATT_TILE), -1)
        q_tiles = [r, r]
        k_tiles = [P - ATT_TILE + np.arange(ATT_TILE), new]
    return _bias_tiles(rel_bias, q_tiles, k_tiles)


def _trunk(x, k_hist, v_hist, s_hist, pool_hist, conv_hist, w, diff_lambda, rel_bias, time_major_ffn):
    B, T, D = x.shape
    P = 0 if k_hist is None else k_hist.shape[2]
    bias = _attention_bias(rel_bias, T, P)
    hist = None
    if k_hist is not None:
        hist = (k_hist.reshape(N_MIX, B, P * HEADS, HEAD_W), v_hist.reshape(N_MIX, B, P * HEADS, HEAD_W), s_hist)
    kv = None
    xb = x.astype(BF16)
    ss, ps, cs = [], [], []
    for l in range(DEPTH):
        if l % 2 == 0:
            m = l // 2
            lam_init = 0.8 - 0.6 * math.exp(-0.3 * l)
            oa, ob, k_all, v_all, s_new = _mixer(
                xb, w["w7"][m], w["lb_logits"], w["a_gain"][m], w["b_gain"][m], diff_lambda[m].astype(F32),
                bias, hist, kv, m, lam_init)
            kv = (k_all, v_all)
            ss.append(s_new)
            pre = ("proj", oa, ob, w["w_out"][m], w["ln1_g"][l], w["ln1_b"][l])
        else:
            p = l // 2
            pre = ("pool", pool_hist[p], w["pool_w"][p], w["pool_scale"][p], w["ln1_g"][l], w["ln1_b"][l], P)
        ffn_w = (w["w_up"][l], w["conv_w"][l], w["conv_b"][l], w["w_down"][l], w["ln2_g"][l], w["ln2_b"][l])
        want_bf16 = (l + 1 < DEPTH) and (l + 1) % 2 == 0
        if time_major_ffn:
            if pre[0] == "proj":
                x = _out_proj_ln(x.reshape(B * T, D), oa.reshape(B * T, A_W), ob.reshape(B * T, A_W),
                                 *pre[3:]).reshape(B, T, D)
            else:
                x, ph = _pool_ln(x, *pre[1:])
                ps.append(ph)
            xt = x.transpose(1, 0, 2).reshape(1, T * B, D)
            ht = conv_hist[l].transpose(1, 0, 2).reshape(1, (CONV_W - 1) * B, D_FF)
            outs, ch = _conv_ffn_ln(xt, ht, *ffn_w, sh=B, emit_bf16=want_bf16)
            outs = [o.reshape(T, B, D).transpose(1, 0, 2) for o in outs]
            ch = ch.reshape(CONV_W - 1, B, D_FF).transpose(1, 0, 2)
        else:
            outs, ch = _conv_ffn_ln(x, conv_hist[l], *ffn_w, sh=1, emit_bf16=want_bf16, pre=pre)
            if pre[0] == "pool":
                ps.append(outs[-1])
        x = outs[0]
        if want_bf16:
            xb = outs[1]
        cs.append(ch)
    k_all = kv[0].reshape(N_MIX, B, T, HEADS, HEAD_W)
    v_all = kv[1].reshape(N_MIX, B, T, HEADS, HEAD_W)
    return x, k_all, v_all, jnp.stack(ss), jnp.stack(ps), jnp.stack(cs)


def kernel(x_prompt, x_sample, cache_k, cache_v, state_hgrn, state_pool, state_ffn_conv, w_in_ab, w_out_ab,
           lb_logits, hgrn_norm_g, diff_lambda, diff_norm_g, rel_bias, pool_w, pool_scale, ffn_w_up,
           ffn_conv_w, ffn_conv_b, ffn_w_down, ln1_g, ln1_b, ln2_g, ln2_b):
    w = _prep_weights(w_in_ab, w_out_ab, lb_logits, hgrn_norm_g, diff_norm_g, pool_w, pool_scale, ffn_w_up,
                      ffn_conv_w, ffn_conv_b, ffn_w_down, ln1_g, ln1_b, ln2_g, ln2_b)
    Bp, Tp, _ = x_prompt.shape
    dt = x_prompt.dtype
    y_p, k_p, v_p, s_p, pool_p, conv_p = _trunk(
        x_prompt, None, None, None,
        jnp.zeros((DEPTH // 2, Bp, POOL_HIST, D_MODEL), dt),
        jnp.zeros((DEPTH, Bp, CONV_W - 1, D_FF), dt),
        w, diff_lambda, rel_bias, time_major_ffn=False)
    y_s, k_s, v_s, s_s, pool_s, conv_s = _trunk(
        x_sample, cache_k, cache_v, state_hgrn, state_pool, state_ffn_conv,
        w, diff_lambda, rel_bias, time_major_ffn=True)
    return (y_p, y_s, k_p, v_p, s_p, pool_p, conv_p, k_s, v_s, s_s, pool_s, conv_s)
```

```python
import functools
import math

import numpy as np
import jax
import jax.numpy as jnp
from jax import lax
from jax.experimental import pallas as pl
from jax.experimental.pallas import tpu as pltpu

F32 = jnp.float32
BF16 = jnp.bfloat16

D_MODEL = 1024
DEPTH = 4
CHUNK = 64
N_MIX = (DEPTH + 1) // 2
HEADS = 4
HEAD_W = 128
A_W = HEADS * HEAD_W
B_DQK = 64
POOL_WINDOWS = (2, 4, 8, 16)
POOL_GROUP = D_MODEL // len(POOL_WINDOWS)
POOL_HIST = max(POOL_WINDOWS) - 1
D_FF = 256 * math.ceil(8 * D_MODEL / 3 / 256)
CONV_W = 3
NUM_BUCKETS = 32
MAX_DISTANCE = 128
DN_ALPHA = (2 * DEPTH) ** 0.25
LN_EPS = 1e-5
RMS_EPS = 1e-6
MASK_NEG = -1e30
LOG2E = 1.0 / math.log(2.0)

VMEM_LIMIT_BYTES = 56 * 1024 * 1024
ATT_TILE = 256
ROW_TILE = 512
FFN_ROW_TILE = 1024
FFN_SUB_TILE = 512
FF_CHUNK = 256
N_PROJ = 7
HGRN_SUB = 4


def _nt(a, b):
    return lax.dot_general(a, b, (((1,), (1,)), ((), ())), preferred_element_type=F32)


def _tn(a, b):
    return lax.dot_general(a, b, (((0,), (0,)), ((), ())), preferred_element_type=F32)


def _mm(a, b):
    return jnp.dot(a, b, preferred_element_type=F32)


def _layer_norm(y, g, b):
    mu = jnp.mean(y, axis=-1, keepdims=True)
    yc = y - mu
    var = jnp.mean(yc * yc, axis=-1, keepdims=True)
    return yc * lax.rsqrt(var + LN_EPS) * g + b


def _rms_norm(y, g):
    return y * lax.rsqrt(jnp.mean(y * y, axis=-1, keepdims=True) + RMS_EPS) * g


def _compiler_params(semantics):
    return pltpu.CompilerParams(dimension_semantics=semantics, vmem_limit_bytes=VMEM_LIMIT_BYTES)


def _rel_bucket(rel):
    half = NUM_BUCKETS // 2
    max_exact = half // 2
    base = jnp.where(rel > 0, half, 0)
    n = jnp.abs(rel)
    nf = jnp.maximum(n, 1).astype(jnp.float32)
    large = max_exact + (jnp.log(nf / max_exact) / math.log(MAX_DISTANCE / max_exact)
                         * (half - max_exact)).astype(jnp.int32)
    large = jnp.minimum(large, half - 1)
    return base + jnp.where(n < max_exact, n, large)


def _bias_tile_body(far_bucket, bucket_ref, rb_ref, out_ref):
    h = pl.program_id(0)
    bucket = bucket_ref[...]
    acc = jnp.zeros(bucket.shape, F32)
    for b in range(NUM_BUCKETS):
        acc = jnp.where(bucket == b, rb_ref[b, h], acc)
    out_ref[...] = jnp.where(bucket < 0, MASK_NEG, (acc - rb_ref[far_bucket, h]) * LOG2E)


def _bias_tiles(rel_bias, q_pos_tiles, k_pos_tiles):
    buckets = []
    for qp, kp in zip(q_pos_tiles, k_pos_tiles):
        qp = jnp.asarray(qp, jnp.int32)
        kp = jnp.asarray(kp, jnp.int32)
        visible = (kp[None, :] // CHUNK) <= (qp[:, None] // CHUNK)
        buckets.append(jnp.where(visible & (kp[None, :] >= 0), _rel_bucket(kp[None, :] - qp[:, None]), -1))
    bucket = jnp.stack(buckets).astype(jnp.int32)
    n, r, c = bucket.shape
    far_bucket = NUM_BUCKETS // 2 - 1
    return pl.pallas_call(
        functools.partial(_bias_tile_body, far_bucket),
        grid=(HEADS, n),
        in_specs=[pl.BlockSpec((None, r, c), lambda h, i: (i, 0, 0)),
                  pl.BlockSpec(memory_space=pltpu.SMEM)],
        out_specs=pl.BlockSpec((None, None, r, c), lambda h, i: (h, i, 0, 0)),
        out_shape=jax.ShapeDtypeStruct((HEADS, n, r, c), F32),
        name="bias_tiles",
    )(bucket, rel_bias.astype(F32))


def _hgrn_constants(L, n_sub):
    levels = []
    w = L // 2
    while w >= 1:
        levels.append(w)
        w //= 2
    t = np.arange(L)[:, None]
    u = np.arange(L)[None, :]
    blocks = [(u <= t)]
    for w in levels:
        start = (t // w) * w
        blocks.append((u > start) & (u <= t))
        nxt = np.minimum((t // w + 1) * w, L - 1)
        blocks.append((u > t) & (u <= nxt))
    g = np.concatenate(blocks, axis=0).astype(np.float32)
    g3 = np.concatenate([g, g, g], axis=1)
    level = np.full((L, L), -1, np.int32)
    s = u
    for i, w in enumerate(levels):
        hit = (t // (2 * w) == s // (2 * w)) & ((t // w) % 2 == 1) & ((s // w) % 2 == 0)
        level[hit] = i
    level[np.arange(L), np.arange(L)] = len(levels)
    pair = np.concatenate([level] * min(n_sub, 2), axis=1)
    return jnp.asarray(g3, BF16), jnp.asarray(pair, jnp.int32), len(levels)


def _diff_softmax_pv(s1, s2, lam, values):
    def probs(ss):
        m = functools.reduce(jnp.maximum, [jnp.max(s, axis=-1, keepdims=True) for s in ss])
        ps = [jnp.exp2(s - m) for s in ss]
        l = functools.reduce(jnp.add, [jnp.sum(p, axis=-1, keepdims=True) for p in ps])
        return ps, l
    p1, l1 = probs(s1)
    p2, l2 = probs(s2)
    c1 = 1.0 / l1
    c2 = lam / l2
    return functools.reduce(jnp.add, [_mm((c1 * a - c2 * b).astype(BF16), v) for a, b, v in zip(p1, p2, values)])


def _mixer_body(cfg, *refs):
    P, m_idx = cfg[1], cfg[4]
    if P == 0:
        _mixer_head(cfg, pl.program_id(1), *refs)
        return
    refs = list(refs)
    n_in = 14
    for hh in range(HEADS):
        cols = pl.ds(hh * HEAD_W, HEAD_W)
        view = list(refs)
        for pos in (1, 2, 3, 6, 11, n_in + 4):
            view[pos] = refs[pos].at[hh]
        for pos in (n_in, n_in + 1):
            view[pos] = refs[pos].at[:, cols]
        _mixer_head(cfg, hh, *view)


def _mixer_head(cfg, h, *refs):
    T, P, L, n_sub, m_idx, lam_init, n_levels = cfg
    has_hist = P > 0
    R = n_sub * L
    (x_ref, w_ref, lbl_ref, ag_ref, bg_ref, dl_ref, bias_ref, g_ref, lev_ref), refs = refs[:9], refs[9:]
    if has_hist:
        (hk_ref, hv_ref, s0_ref), refs = refs[:3], refs[3:]
    refs = refs[2:]
    (oa_ref, ob_ref, k_ref, v_ref, s_ref, proj_ref, st_ref, q1_ref, q2_ref, kb_ref, vb_ref), refs = refs[:11], refs[11:]
    n_tiles = T // R
    lane = lax.broadcasted_iota(jnp.int32, (1, HEAD_W), 1)

    def run(jobs):
        jobs = list(jobs)
        while jobs:
            for job in list(jobs):
                try:
                    next(job)
                except StopIteration:
                    jobs.remove(job)

    def project(i):
        rows = slice(i * R, (i + 1) * R)
        for c0 in range(0, N_PROJ * HEAD_W, 2 * HEAD_W):
            c1 = min(c0 + 2 * HEAD_W, N_PROJ * HEAD_W)
            proj = _mm(x_ref[rows, :], w_ref[:, c0:c1])
            proj_ref[rows, c0:c1] = proj
            for k in range(c0 // HEAD_W, c1 // HEAD_W):
                blk = proj[:, k * HEAD_W - c0:(k + 1) * HEAD_W - c0]
                if k == 4:
                    qs = blk * (B_DQK ** -0.5 * LOG2E)
                    q1_ref[rows, :] = jnp.where(lane < B_DQK, qs, 0.0).astype(BF16)
                    q2_ref[rows, :] = jnp.where(lane >= B_DQK, qs, 0.0).astype(BF16)
                elif k == 5:
                    kb_ref[rows, :] = blk.astype(BF16)
                elif k == 6:
                    vb_ref[rows, :] = blk.astype(BF16)
            yield

    lg = lbl_ref[...]
    e = jnp.exp(lg - jnp.max(lg, axis=0, keepdims=True))
    sm = e / jnp.sum(e, axis=0, keepdims=True)
    lb = jnp.sum(sm[0:m_idx + 1], axis=0, keepdims=True) - sm[0:1]
    a_gain = ag_ref[...]

    if has_hist:
        st_ref[...] = s0_ref[...].T
    else:
        st_ref[...] = jnp.zeros((HEAD_W, HEAD_W), F32)
    level = lev_ref[...]

    def lanes(j, n=1):
        return slice(j * HEAD_W, (j + n) * HEAD_W)

    def side_by_side(parts):
        return parts[0] if len(parts) == 1 else jnp.concatenate(parts, axis=1)

    lb_w = side_by_side([lb] * n_sub)
    zero_blk = jnp.zeros((L, HEAD_W), BF16)
    groups = [list(range(j, min(j + 2, n_sub))) for j in range(0, n_sub, 2)]

    def block_diag(a, js):
        if len(js) == 1:
            return a[:, lanes(js[0])]
        return jnp.concatenate(
            [side_by_side([a[:, lanes(j)] if jj == j else zero_blk for jj in js]) for j in js], axis=0)

    def hgrn_step(c):
        def stream(k):
            return side_by_side([proj_ref[c * R + j * L:c * R + (j + 1) * L, lanes(k)] for j in range(n_sub)])

        q = stream(0)
        zf = stream(1)
        vi = stream(2).astype(BF16)
        gate = stream(3)
        log_f = jnp.log(lb_w + (1.0 - lb_w) * jax.nn.sigmoid(zf))
        k_in = (1.0 - lb_w) * jax.nn.sigmoid(-zf)
        hi = log_f.astype(BF16)
        r1 = log_f - hi.astype(F32)
        mid = r1.astype(BF16)
        lo = (r1 - mid.astype(F32)).astype(BF16)
        ex = _mm(g_ref[...], jnp.concatenate([hi, mid, lo], axis=0))
        b = ex[0:L]
        yield

        def pair_scores(qw, kw, js):
            return _nt(qw[:, lanes(js[0], len(js))], block_diag(kw, js))

        scores = [jnp.zeros((L, len(js) * L), F32) for js in groups]
        for i in range(n_levels + 1):
            if i < n_levels:
                qw = (q * jnp.exp(ex[(1 + 2 * i) * L:(2 + 2 * i) * L])).astype(BF16)
                kw = (k_in * jnp.exp(ex[(2 + 2 * i) * L:(3 + 2 * i) * L])).astype(BF16)
            else:
                qw, kw = q.astype(BF16), k_in.astype(BF16)
            for gi, js in enumerate(groups):
                lev = level if len(js) == level.shape[1] // L else level[:, 0:len(js) * L]
                scores[gi] = jnp.where(lev == i, pair_scores(qw, kw, js), scores[gi])
            if i % 2 == 1:
                yield
        o = _mm(side_by_side(scores).astype(BF16), block_diag(vi, list(range(n_sub))))
        qd = (q * jnp.exp(b)).astype(BF16)
        st = st_ref[...]
        o_state = []
        for j in range(n_sub):
            o_state.append(_nt(qd[:, lanes(j)], st.astype(BF16)))
            b_last = b[L - 1:L, lanes(j)]
            kd = (k_in[:, lanes(j)] * jnp.exp(b_last - b[:, lanes(j)])).astype(BF16)
            st = st * jnp.exp(b_last) + _tn(vi[:, lanes(j)], kd)
            if j % 2 == 1:
                yield
        st_ref[...] = st
        o = o + side_by_side(o_state)
        act = gate * jax.nn.sigmoid(gate)
        for j in range(n_sub):
            oa_ref[c * R + j * L:c * R + (j + 1) * L, :] = (
                _rms_norm(o[:, lanes(j)], a_gain) * act[:, lanes(j)]).astype(BF16)

    lp = dl_ref[...]
    lam = (jnp.exp(jnp.sum(lp[0:1] * lp[1:2], axis=-1, keepdims=True))
           - jnp.exp(jnp.sum(lp[2:3] * lp[3:4], axis=-1, keepdims=True)) + lam_init)
    b_gain = bg_ref[...] * (1.0 - lam_init)

    def attend(i):
        segs = []
        if i >= 2:
            segs.append((0, (i - 1) * R, None))
        if i >= 1:
            segs.append(((i - 1) * R, i * R, 1))
        segs.append((i * R, (i + 1) * R, 0))

        rows = slice(i * R, (i + 1) * R)
        stats = []
        for q_ref in (q1_ref, q2_ref):
            q = q_ref[rows, :]
            ss = []
            for lo_k, hi_k, kind in segs:
                s = _nt(q, kb_ref[lo_k:hi_k, :])
                ss.append(s if kind is None else s + bias_ref[kind])
            m = functools.reduce(jnp.maximum, [jnp.max(s, axis=-1, keepdims=True) for s in ss])
            yield
            ps = []
            for s in ss:
                ps.append(jnp.exp2(s - m))
                yield
            stats.append((ps, functools.reduce(jnp.add, [jnp.sum(p, axis=-1, keepdims=True) for p in ps])))
        (p1, l1), (p2, l2) = stats
        c1 = 1.0 / l1
        c2 = lam / l2
        o = None
        for a, b, (lo_k, hi_k, _) in zip(p1, p2, segs):
            part = _mm((c1 * a - c2 * b).astype(BF16), vb_ref[lo_k:hi_k, :])
            o = part if o is None else o + part
            yield
        ob_ref[rows, :] = (_rms_norm(o, 1.0) * b_gain).astype(BF16)

    def store_kv(hh):
        k_ref[pl.ds(hh, T, stride=HEADS), :] = proj_ref[:, 5 * HEAD_W:6 * HEAD_W]
        v_ref[pl.ds(hh, T, stride=HEADS), :] = proj_ref[:, 6 * HEAD_W:7 * HEAD_W]

    if has_hist:
        kd_ref, vd_ref = refs
        kb_ref[...] = jnp.zeros(kb_ref.shape, BF16)
        vb_ref[...] = jnp.zeros(vb_ref.shape, BF16)

        def gather_cache():
            piece = P // 4
            for src, dst in ((hk_ref, kd_ref), (hv_ref, vd_ref)):
                for r0 in range(0, P, piece):
                    dst[r0:r0 + piece, :] = src[pl.ds(h + r0 * HEADS, piece, stride=HEADS), :].astype(BF16)
                    yield

    run([project(0)])
    for i in range(n_tiles):
        jobs = [project(i + 1)] if i + 1 < n_tiles else []
        jobs.append(hgrn_step(i))
        jobs.append(gather_cache() if has_hist else attend(i))
        run(jobs)
    s_ref[...] = st_ref[...].T

    if has_hist:
        store_kv(h)
    else:
        for hh in range(HEADS):
            pl.when(h == hh)(functools.partial(store_kv, hh))

    if has_hist:
        tk = ATT_TILE
        bias_new = bias_ref[1, :, 0:HEAD_W]

        def scores_of(q):
            return [_nt(q, kd_ref[0:P - tk, :]), _nt(q, kd_ref[P - tk:P, :]) + bias_ref[0],
                    _nt(q, kb_ref[...]) + bias_new]

        o = _diff_softmax_pv(scores_of(q1_ref[...]), scores_of(q2_ref[...]), lam,
                             [vd_ref[0:P - tk, :], vd_ref[P - tk:P, :], vb_ref[...]])
        ob_ref[...] = (_rms_norm(o, 1.0) * b_gain).astype(BF16)


def _mixer(x, w7, lb_logits, a_gain, b_gain, dlam, bias, hist, kv_prev, m_idx, lam_init):
    B, T, D = x.shape
    L = min(T, CHUNK)
    n_sub = min(T // L, HGRN_SUB)
    gmat, level, n_levels = _hgrn_constants(L, n_sub)
    P = 0 if hist is None else hist[0].shape[2] // HEADS
    assert T % (n_sub * L) == 0 and T % min(T, ROW_TILE) == 0 and T % min(T, ATT_TILE) == 0
    assert ATT_TILE > MAX_DISTANCE and (T <= HEAD_W if hist is not None else True)
    assert P % ATT_TILE == 0 and (P == 0 or P >= 2 * ATT_TILE)
    assert hist is not None or n_sub * L == min(T, ATT_TILE)
    cfg = (T, P, L, n_sub, m_idx, lam_init, n_levels)
    wcols = N_PROJ * HEAD_W
    per_head = hist is None
    hd = None if per_head else HEADS
    hw = HEAD_W if per_head else A_W
    grid = (B, HEADS if per_head else 1)
    assert per_head or T == n_sub * L
    in_specs = [
        pl.BlockSpec((None, T, D), lambda b, h: (b, 0, 0)),
        pl.BlockSpec((hd, D, wcols), lambda b, h: (h, 0, 0)),
        pl.BlockSpec((hd, N_MIX, HEAD_W), lambda b, h: (h, 0, 0)),
        pl.BlockSpec((hd, 1, HEAD_W), lambda b, h: (h, 0, 0)),
        pl.BlockSpec((1, HEAD_W), lambda b, h: (0, 0)),
        pl.BlockSpec(dlam.shape, lambda b, h: (0, 0)),
        pl.BlockSpec((hd,) + bias.shape[1:], lambda b, h: (h, 0, 0, 0)),
        pl.BlockSpec(gmat.shape, lambda b, h: (0, 0)),
        pl.BlockSpec(level.shape, lambda b, h: (0, 0)),
    ]
    args = [x, w7, lb_logits, a_gain, b_gain, dlam, bias, gmat, level]
    if hist is not None:
        hk, hv, s0 = hist
        in_specs += [
            pl.BlockSpec((None, None, P * HEADS, HEAD_W), lambda b, h: (m_idx, b, 0, 0)),
            pl.BlockSpec((None, None, P * HEADS, HEAD_W), lambda b, h: (m_idx, b, 0, 0)),
            pl.BlockSpec((None, None, HEADS, HEAD_W, HEAD_W), lambda b, h: (m_idx, b, 0, 0, 0)),
        ]
        args += [hk, hv, s0]
    aliases = {len(args): 2, len(args) + 1: 3}
    in_specs += [pl.BlockSpec(memory_space=pl.ANY), pl.BlockSpec(memory_space=pl.ANY)]
    args += list(kv_prev)
    col_spec = pl.BlockSpec((None, T, hw), lambda b, h: (b, 0, h))
    kv_spec = pl.BlockSpec((None, None, T * HEADS, HEAD_W), lambda b, h: (m_idx, b, 0, 0))
    kv_shape = jax.ShapeDtypeStruct((N_MIX, B, T * HEADS, HEAD_W), F32)
    key_rows = T if hist is None else HEAD_W
    scratch = [pltpu.VMEM((T, wcols), F32), pltpu.VMEM((HEAD_W, HEAD_W), F32),
               pltpu.VMEM((T, HEAD_W), BF16), pltpu.VMEM((T, HEAD_W), BF16),
               pltpu.VMEM((key_rows, HEAD_W), BF16), pltpu.VMEM((key_rows, HEAD_W), BF16)]
    if hist is not None:
        scratch += [pltpu.VMEM((P, HEAD_W), BF16), pltpu.VMEM((P, HEAD_W), BF16)]
    return pl.pallas_call(
        functools.partial(_mixer_body, cfg),
        grid=grid,
        in_specs=in_specs,
        out_specs=[col_spec, col_spec, kv_spec, kv_spec,
                   pl.BlockSpec((None, hd, HEAD_W, HEAD_W), lambda b, h: (b, h, 0, 0))],
        out_shape=[jax.ShapeDtypeStruct((B, T, A_W), BF16), jax.ShapeDtypeStruct((B, T, A_W), BF16),
                   kv_shape, kv_shape, jax.ShapeDtypeStruct((B, HEADS, HEAD_W, HEAD_W), F32)],
        scratch_shapes=scratch,
        input_output_aliases=aliases,
        compiler_params=_compiler_params(("arbitrary", "arbitrary")),
        name="mixer_ab",
    )(*args)


def _out_proj_body(x_ref, oa_ref, ob_ref, w_ref, g_ref, b_ref, out_ref):
    y = _mm(oa_ref[...], w_ref[0:A_W, :]) + _mm(ob_ref[...], w_ref[A_W:2 * A_W, :])
    out_ref[...] = _layer_norm(DN_ALPHA * x_ref[...] + y, g_ref[...], b_ref[...])


def _out_proj_ln(x, oa, ob, w_out, g, b):
    N, D = x.shape
    tm = min(N, ROW_TILE)
    row = lambda w: pl.BlockSpec((tm, w), lambda i: (i, 0))
    full = lambda a: pl.BlockSpec(a.shape, lambda i: (0,) * a.ndim)
    return pl.pallas_call(
        _out_proj_body,
        grid=(N // tm,),
        in_specs=[row(D), row(A_W), row(A_W), full(w_out), full(g), full(b)],
        out_specs=row(D),
        out_shape=jax.ShapeDtypeStruct((N, D), F32),
        compiler_params=_compiler_params(("arbitrary",)),
        name="out_proj_ln",
    )(x, oa, ob, w_out, g, b)


def _pool_body(cfg, x_ref, prev_ref, hist_ref, pw_ref, sc_ref, g_ref, b_ref, out_ref, ph_ref, xs_ref):
    tm, pos0 = cfg
    H = POOL_HIST + 1
    i = pl.program_id(1)
    x = x_ref[...]
    xs_ref[H:H + tm, :] = x

    @pl.when(i == 0)
    def _():
        xs_ref[0:1, :] = jnp.zeros((1, D_MODEL), F32)
        xs_ref[1:H, :] = hist_ref[...]

    @pl.when(i > 0)
    def _():
        xs_ref[0:H, :] = prev_ref[...]

    pos = pos0 + i * tm + lax.broadcasted_iota(jnp.int32, (tm, 1), 0)
    ys = []
    for g, w in enumerate(POOL_WINDOWS):
        cols = slice(g * POOL_GROUP, (g + 1) * POOL_GROUP)
        s = x[:, cols]
        for k in range(1, w):
            s = s + xs_ref[H - k:H - k + tm, cols]
        cnt = jnp.minimum(w, pos + 1).astype(F32)
        pooled = s / cnt - x[:, cols]
        ys.append(_mm(pooled.astype(BF16), pw_ref[g]))
    y = jnp.concatenate(ys, axis=-1) * sc_ref[...]
    out_ref[...] = _layer_norm(DN_ALPHA * x + y, g_ref[...], b_ref[...])
    ph_ref[...] = xs_ref[tm + 1:tm + H, :]


def _pool_ln(x, hist, pool_w, scale, g, b, pos0):
    B, T, D = x.shape
    H = POOL_HIST + 1
    tm = min(T, ROW_TILE)
    per = tm // H
    full = lambda a: pl.BlockSpec(a.shape, lambda bi, i: (0,) * a.ndim)
    return pl.pallas_call(
        functools.partial(_pool_body, (tm, pos0)),
        grid=(B, T // tm),
        in_specs=[pl.BlockSpec((None, tm, D), lambda bi, i: (bi, i, 0)),
                  pl.BlockSpec((None, H, D), lambda bi, i: (bi, jnp.maximum(i * per - 1, 0), 0)),
                  pl.BlockSpec((None, POOL_HIST, D), lambda bi, i: (bi, 0, 0)),
                  full(pool_w), full(scale), full(g), full(b)],
        out_specs=[pl.BlockSpec((None, tm, D), lambda bi, i: (bi, i, 0)),
                   pl.BlockSpec((None, POOL_HIST, D), lambda bi, i: (bi, 0, 0))],
        out_shape=[jax.ShapeDtypeStruct((B, T, D), F32), jax.ShapeDtypeStruct((B, POOL_HIST, D), F32)],
        scratch_shapes=[pltpu.VMEM((H + tm, D), F32)],
        compiler_params=_compiler_params(("arbitrary", "arbitrary")),
        name="pool_ln",
    )(x, x, hist, pool_w, scale, g, b)


def _pooled_rows(x, halo, pos, pw_ref, scale):
    H = POOL_HIST + 1
    ys = []
    for g, w in enumerate(POOL_WINDOWS):
        cols = slice(g * POOL_GROUP, (g + 1) * POOL_GROUP)
        xg = x[:, cols]
        sg = xg
        eg = jnp.concatenate([halo[:, cols], xg[0:H]], axis=0)
        d = 1
        while d < w:
            sg = sg + pltpu.roll(sg, d, 0)
            eg = eg + pltpu.roll(eg, d, 0)
            d *= 2
        sg = jnp.concatenate([eg[H:2 * H], sg[H:]], axis=0)
        cnt = jnp.minimum(w, pos + 1).astype(F32)
        pooled = sg / cnt - xg
        ys.append(_mm(pooled.astype(BF16), pw_ref[g]))
    return jnp.concatenate(ys, axis=-1) * scale


def _ffn_body(cfg, *refs):
    tm, sh, emit_bf16, pre, pos0 = cfg
    x_ref, hist_ref, wup_ref, cw_ref, cb_ref, wdn_ref, g_ref, b_ref = refs[:8]
    refs = refs[8:]
    if pre == "proj":
        (oa_ref, ob_ref, wo_ref, g1_ref, b1_ref), refs = refs[:5], refs[5:]
    elif pre == "pool":
        (prev_ref, ph_in_ref, pw_ref, sc_ref, g1_ref, b1_ref), refs = refs[:6], refs[6:]
    out_ref, refs = refs[0], refs[1:]
    if emit_bf16:
        outb_ref, refs = refs[0], refs[1:]
    if pre == "pool":
        ph_ref, refs = refs[0], refs[1:]
    cs_ref, carry_ref, gbuf_ref = refs
    halo = (CONV_W - 1) * sh
    pad = -(-halo // 8) * 8
    top = max(pad, 16)
    n_chunks = D_FF // FF_CHUNK
    H = POOL_HIST + 1
    i = pl.program_id(1)

    @pl.when(i == 0)
    def _():
        carry_ref[...] = hist_ref[...]

    def conv_act(a_cur, shifted, gate, cols):
        conv = cb_ref[:, cols]
        for j in range(CONV_W - 1):
            conv = conv + shifted(j) * cw_ref[j:j + 1, cols]
        conv = conv + a_cur * cw_ref[CONV_W - 1:CONV_W, cols]
        return (conv * jax.nn.sigmoid(conv) * gate).astype(BF16)

    n_sub = tm // FFN_SUB_TILE if tm % FFN_SUB_TILE == 0 else 1
    ts = tm // n_sub
    for s in range(n_sub):
        rows = slice(s * ts, (s + 1) * ts)
        x = x_ref[rows, :]
        if pre == "proj":
            y = _mm(oa_ref[rows, :], wo_ref[0:A_W, :]) + _mm(ob_ref[rows, :], wo_ref[A_W:2 * A_W, :])
            x = _layer_norm(DN_ALPHA * x + y, g1_ref[...], b1_ref[...])
        elif pre == "pool":
            if s == 0:
                before = jnp.where(i == 0, ph_in_ref[...], prev_ref[...])
            else:
                before = x_ref[s * ts - H:s * ts, :]
            pos = pos0 + i * tm + s * ts + lax.broadcasted_iota(jnp.int32, (ts, 1), 0)
            y = _pooled_rows(x, before, pos, pw_ref, sc_ref[...])
            x = _layer_norm(DN_ALPHA * x + y, g1_ref[...], b1_ref[...])
        xb = x.astype(BF16)
        for c in range(n_chunks):
            cols = slice(c * FF_CHUNK, (c + 1) * FF_CHUNK)
            a = _mm(xb, wup_ref[:, cols])
            gate = _mm(xb, wup_ref[:, D_FF + c * FF_CHUNK:D_FF + (c + 1) * FF_CHUNK])
            gbuf_ref[rows, cols] = conv_act(a, lambda j: pltpu.roll(a, (CONV_W - 1 - j) * sh, 0), gate, cols)
            ext = jnp.concatenate([carry_ref[:, cols], a[0:top]], axis=0)
            gbuf_ref[s * ts:s * ts + top, cols] = conv_act(
                a[0:top], lambda j: pltpu.roll(ext, (CONV_W - 1 - j) * sh, 0)[pad:pad + top], gate[0:top], cols)
            carry_ref[:, cols] = a[ts - pad:ts]
        f = _mm(gbuf_ref[rows, :], wdn_ref[...])
        y = _layer_norm(DN_ALPHA * x + f, g_ref[...], b_ref[...])
        out_ref[rows, :] = y
        if emit_bf16:
            outb_ref[rows, :] = y.astype(BF16)
    cs_ref[...] = carry_ref[...]
    if pre == "pool":
        ph_ref[...] = x_ref[tm - POOL_HIST:tm, :]


def _conv_ffn_ln(x, hist, w_up, conv_w, conv_b, w_down, g, b, sh, emit_bf16, pre=None):
    B, T, D = x.shape
    halo = (CONV_W - 1) * sh
    tm = min(T, FFN_ROW_TILE)
    pad = -(-halo // 8) * 8
    H = POOL_HIST + 1
    assert T % tm == 0 and tm >= max(pad, 16)
    hist_p = jnp.pad(hist, ((0, 0), (pad - halo, 0), (0, 0)))
    full = lambda a: pl.BlockSpec(a.shape, lambda bi, i: (0,) * a.ndim)
    row = lambda w: pl.BlockSpec((None, tm, w), lambda bi, i: (bi, i, 0))
    hist_spec = pl.BlockSpec((None, pad, D_FF), lambda bi, i: (bi, 0, 0))
    kind, pos0, pre_args, pre_specs = None, 0, [], []
    if pre is not None:
        kind = pre[0]
        if kind == "proj":
            _, oa, ob, w_out, g1, b1 = pre
            pre_args = [oa, ob, w_out, g1, b1]
            pre_specs = [row(A_W), row(A_W), full(w_out), full(g1), full(b1)]
        else:
            _, pool_hist, pool_w, scale, g1, b1, pos0 = pre
            assert sh == 1 and tm % H == 0
            ph_in = jnp.pad(pool_hist, ((0, 0), (1, 0), (0, 0)))
            pre_args = [x, ph_in, pool_w, scale, g1, b1]
            pre_specs = [pl.BlockSpec((None, H, D), lambda bi, i: (bi, jnp.maximum(i * (tm // H) - 1, 0), 0)),
                         pl.BlockSpec((None, H, D), lambda bi, i: (bi, 0, 0)),
                         full(pool_w), full(scale), full(g1), full(b1)]
    out_specs = [row(D)] + ([row(D)] if emit_bf16 else [])
    out_shape = [jax.ShapeDtypeStruct((B, T, D), F32)] + ([jax.ShapeDtypeStruct((B, T, D), BF16)] if emit_bf16 else [])
    if kind == "pool":
        out_specs.append(pl.BlockSpec((None, POOL_HIST, D), lambda bi, i: (bi, 0, 0)))
        out_shape.append(jax.ShapeDtypeStruct((B, POOL_HIST, D), F32))
    out_specs.append(hist_spec)
    out_shape.append(jax.ShapeDtypeStruct((B, pad, D_FF), F32))
    res = pl.pallas_call(
        functools.partial(_ffn_body, (tm, sh, emit_bf16, kind, pos0)),
        grid=(B, T // tm),
        in_specs=[row(D), hist_spec,
                  full(w_up), full(conv_w), full(conv_b), full(w_down), full(g), full(b)] + pre_specs,
        out_specs=out_specs,
        out_shape=out_shape,
        scratch_shapes=[pltpu.VMEM((pad, D_FF), F32), pltpu.VMEM((tm, D_FF), BF16)],
        compiler_params=_compiler_params(("arbitrary", "arbitrary")),
        name="conv_ffn_ln",
    )(x, hist_p, w_up, conv_w, conv_b, w_down, g, b, *pre_args)
    return res[:-1], res[-1][:, pad - halo:, :]


def _prep_weights(w_in_ab, w_out_ab, lb_logits, hgrn_norm_g, diff_norm_g, pool_w, pool_scale, ffn_w_up,
                  ffn_conv_w, ffn_conv_b, ffn_w_down, ln1_g, ln1_b, ln2_g, ln2_b):
    w = {}
    w["w7"] = (w_in_ab.astype(BF16).reshape(N_MIX, D_MODEL, N_PROJ, HEADS, HEAD_W)
               .transpose(0, 3, 1, 2, 4).reshape(N_MIX, HEADS, D_MODEL, N_PROJ * HEAD_W))
    w["w_out"] = w_out_ab.astype(BF16)
    w["lb_logits"] = lb_logits.astype(F32).reshape(N_MIX, HEADS, HEAD_W).transpose(1, 0, 2)
    w["a_gain"] = hgrn_norm_g.astype(F32).reshape(N_MIX, HEADS, 1, HEAD_W)
    w["b_gain"] = diff_norm_g.astype(F32).reshape(N_MIX, 1, HEAD_W)
    w["pool_w"] = pool_w.astype(BF16)
    w["pool_scale"] = pool_scale.astype(F32).reshape(-1, 1, D_MODEL)
    w["w_up"] = ffn_w_up.astype(BF16)
    w["conv_w"] = ffn_conv_w.astype(F32)
    w["conv_b"] = ffn_conv_b.astype(F32).reshape(DEPTH, 1, D_FF)
    w["w_down"] = ffn_w_down.astype(BF16)
    for name, v in (("ln1_g", ln1_g), ("ln1_b", ln1_b), ("ln2_g", ln2_g), ("ln2_b", ln2_b)):
        w[name] = v.astype(F32).reshape(DEPTH, 1, D_MODEL)
    return w


def _attention_bias(rel_bias, T, P):
    if P == 0:
        tq = min(T, ATT_TILE)
        r = np.arange(tq)
        q_tiles = [tq + r] * 2
        k_tiles = [tq + r, r]
    else:
        r = P + np.arange(T)
        new = np.where(np.arange(ATT_TILE) < T, P + np.arange(ATT_TILE), -1)
        q_tiles = [r, r]
        k_tiles = [P - ATT_TILE + np.arange(ATT_TILE), new]
    return _bias_tiles(rel_bias, q_tiles, k_tiles)


def _trunk(x, k_hist, v_hist, s_hist, pool_hist, conv_hist, w, diff_lambda, rel_bias, time_major_ffn):
    B, T, D = x.shape
    P = 0 if k_hist is None else k_hist.shape[2]
    bias = _attention_bias(rel_bias, T, P)
    hist = None
    if k_hist is not None:
        hist = (k_hist.reshape(N_MIX, B, P * HEADS, HEAD_W), v_hist.reshape(N_MIX, B, P * HEADS, HEAD_W), s_hist)
    kv = (jnp.zeros((N_MIX, B, T * HEADS, HEAD_W), F32), jnp.zeros((N_MIX, B, T * HEADS, HEAD_W), F32))
    xb = x.astype(BF16)
    ss, ps, cs = [], [], []
    for l in range(DEPTH):
        if l % 2 == 0:
            m = l // 2
            lam_init = 0.8 - 0.6 * math.exp(-0.3 * l)
            oa, ob, k_all, v_all, s_new = _mixer(
                xb, w["w7"][m], w["lb_logits"], w["a_gain"][m], w["b_gain"][m], diff_lambda[m].astype(F32),
                bias, hist, kv, m, lam_init)
            kv = (k_all, v_all)
            ss.append(s_new)
            pre = ("proj", oa, ob, w["w_out"][m], w["ln1_g"][l], w["ln1_b"][l])
        else:
            p = l // 2
            pre = ("pool", pool_hist[p], w["pool_w"][p], w["pool_scale"][p], w["ln1_g"][l], w["ln1_b"][l], P)
        ffn_w = (w["w_up"][l], w["conv_w"][l], w["conv_b"][l], w["w_down"][l], w["ln2_g"][l], w["ln2_b"][l])
        want_bf16 = (l + 1 < DEPTH) and (l + 1) % 2 == 0
        if time_major_ffn:
            if pre[0] == "proj":
                x = _out_proj_ln(x.reshape(B * T, D), oa.reshape(B * T, A_W), ob.reshape(B * T, A_W),
                                 *pre[3:]).reshape(B, T, D)
            else:
                x, ph = _pool_ln(x, *pre[1:])
                ps.append(ph)
            xt = x.transpose(1, 0, 2).reshape(1, T * B, D)
            ht = conv_hist[l].transpose(1, 0, 2).reshape(1, (CONV_W - 1) * B, D_FF)
            outs, ch = _conv_ffn_ln(xt, ht, *ffn_w, sh=B, emit_bf16=want_bf16)
            outs = [o.reshape(T, B, D).transpose(1, 0, 2) for o in outs]
            ch = ch.reshape(CONV_W - 1, B, D_FF).transpose(1, 0, 2)
        else:
            outs, ch = _conv_ffn_ln(x, conv_hist[l], *ffn_w, sh=1, emit_bf16=want_bf16, pre=pre)
            if pre[0] == "pool":
                ps.append(outs[-1])
        x = outs[0]
        if want_bf16:
            xb = outs[1]
        cs.append(ch)
    k_all = kv[0].reshape(N_MIX, B, T, HEADS, HEAD_W)
    v_all = kv[1].reshape(N_MIX, B, T, HEADS, HEAD_W)
    return x, k_all, v_all, jnp.stack(ss), jnp.stack(ps), jnp.stack(cs)


def kernel(x_prompt, x_sample, cache_k, cache_v, state_hgrn, state_pool, state_ffn_conv, w_in_ab, w_out_ab,
           lb_logits, hgrn_norm_g, diff_lambda, diff_norm_g, rel_bias, pool_w, pool_scale, ffn_w_up,
           ffn_conv_w, ffn_conv_b, ffn_w_down, ln1_g, ln1_b, ln2_g, ln2_b):
    w = _prep_weights(w_in_ab, w_out_ab, lb_logits, hgrn_norm_g, diff_norm_g, pool_w, pool_scale, ffn_w_up,
                      ffn_conv_w, ffn_conv_b, ffn_w_down, ln1_g, ln1_b, ln2_g, ln2_b)
    Bp, Tp, _ = x_prompt.shape
    dt = x_prompt.dtype
    y_p, k_p, v_p, s_p, pool_p, conv_p = _trunk(
        x_prompt, None, None, None,
        jnp.zeros((DEPTH // 2, Bp, POOL_HIST, D_MODEL), dt),
        jnp.zeros((DEPTH, Bp, CONV_W - 1, D_FF), dt),
        w, diff_lambda, rel_bias, time_major_ffn=False)
    y_s, k_s, v_s, s_s, pool_s, conv_s = _trunk(
        x_sample, cache_k, cache_v, state_hgrn, state_pool, state_ffn_conv,
        w, diff_lambda, rel_bias, time_major_ffn=True)
    return (y_p, y_s, k_p, v_p, s_p, pool_p, conv_p, k_s, v_s, s_s, pool_s, conv_s)
```
